```python
import math
import jax, jax.numpy as jnp
from jax import lax
import numpy as np

D_MODEL = 1024
BATCH = 4
SEQ = 8192
DEPTH = 2

GRID_W = 64
CTX_LEN = 256
N_MIXERS = 2
N_HEADS = 16
HEAD_DIM = D_MODEL // N_HEADS
NA_KH = 8
NA_KW = 16
SWA_KV_HEADS = 4
SWA_GROUP = N_HEADS // SWA_KV_HEADS
SWA_WINDOW = 128
SWA_BLOCK = 128
D_FF = 4 * D_MODEL
ROPE_BASE = 10000.0
NORM_EPS = 1e-6
NEG_INF = -1e30
N_NA_LAYERS = (DEPTH + 1) // 2
N_SWA_LAYERS = DEPTH // 2

kernel_name = "hybrid_natten_swa_prefix_dit"


def rms_norm(x, g):
    xf = x.astype(jnp.float32)
    y = xf * lax.rsqrt(jnp.mean(xf * xf, axis=-1, keepdims=True) + NORM_EPS)
    return (y * g.astype(jnp.float32)).astype(x.dtype)


def modulate(h, shift, scale):
    return h * (1 + scale) + shift


def squared_relu_mlp(h, w1, w2):
    return jnp.square(jax.nn.relu(h @ w1)) @ w2


def axial_rope_tables(L):
    t = jnp.arange(L, dtype=jnp.int32)
    row = (t // GRID_W).astype(jnp.float32)
    col = (t % GRID_W).astype(jnp.float32)
    n_freq = HEAD_DIM // 4
    inv = ROPE_BASE ** (-jnp.arange(n_freq, dtype=jnp.float32) / n_freq)
    ang = jnp.stack([row[:, None] * inv, col[:, None] * inv], axis=1)
    return jnp.cos(ang), jnp.sin(ang)


def apply_axial_rope(x, cos, sin):
    B, L, h, dh = x.shape
    xr = x.astype(jnp.float32).reshape(B, L, h, 2, 2, dh // 4)
    x1, x2 = xr[..., 0, :], xr[..., 1, :]
    cb = cos[None, :, None]
    sb = sin[None, :, None]
    out = jnp.stack([x1 * cb - x2 * sb, x1 * sb + x2 * cb], axis=-2)
    return out.reshape(B, L, h, dh).astype(x.dtype)


def context_attention(q, k, v, sink=None):
    B, C, KV, G, dh = q.shape
    s = jnp.einsum('bqkgd,bskd->bkgqs', q, k, preferred_element_type=jnp.float32) * (dh ** -0.5)
    if sink is not None:
        s_sink = jnp.broadcast_to(sink.reshape(1, KV, G, 1, 1).astype(jnp.float32), (B, KV, G, C, 1))
        s = jnp.concatenate([s, s_sink], axis=-1)
    p = jax.nn.softmax(s, axis=-1)[..., :C].astype(v.dtype)
    o = jnp.einsum('bkgqs,bskd->bqkgd', p, v)
    return o.reshape(B, C, KV * G * dh)


def neighborhood_attention(q, k, v, kc, vc, rpb):
    B, L, H, dh = q.shape
    rows = L // GRID_W
    kh = min(NA_KH, rows)
    kw = min(NA_KW, GRID_W)
    n_loc = kh * kw
    scale = dh ** -0.5
    qg = q.reshape(B, rows, GRID_W, H, dh)
    kg = k.reshape(B, rows, GRID_W, H, dh)
    vg = v.reshape(B, rows, GRID_W, H, dh)
    cols = jnp.arange(GRID_W)
    col_start = jnp.clip(cols - kw // 2, 0, GRID_W - kw)
    col_idx = col_start[:, None] + jnp.arange(kw)[None, :]
    col_off = col_idx - cols[:, None] + (NA_KW - 1)

    def one_row(r):
        r0 = jnp.clip(r - kh // 2, 0, rows - kh)
        k_rows = lax.dynamic_slice_in_dim(kg, r0, kh, axis=1)
        v_rows = lax.dynamic_slice_in_dim(vg, r0, kh, axis=1)
        k_win = k_rows[:, :, col_idx]
        v_win = v_rows[:, :, col_idx]
        q_row = lax.dynamic_index_in_dim(qg, r, axis=1, keepdims=False)
        s_loc = jnp.einsum('bqhd,bkqwhd->bhqkw', q_row, k_win,
                           preferred_element_type=jnp.float32) * scale
        row_off = r0 + jnp.arange(kh) - r + (NA_KH - 1)
        bias = rpb[:, row_off[:, None, None], col_off[None, :, :]]
        s_loc = s_loc + jnp.transpose(bias, (0, 2, 1, 3)).astype(jnp.float32)[None]
        s_ctx = jnp.einsum('bqhd,bchd->bhqc', q_row, kc,
                           preferred_element_type=jnp.float32) * scale
        s = jnp.concatenate([s_loc.reshape(B, H, GRID_W, n_loc), s_ctx], axis=-1)
        p = jax.nn.softmax(s, axis=-1).astype(v.dtype)
        p_loc = p[..., :n_loc].reshape(B, H, GRID_W, kh, kw)
        p_ctx = p[..., n_loc:]
        return (jnp.einsum('bhqkw,bkqwhd->bqhd', p_loc, v_win)
                + jnp.einsum('bhqc,bchd->bqhd', p_ctx, vc))

    out = lax.map(one_row, jnp.arange(rows))
    return jnp.transpose(out, (1, 0, 2, 3, 4)).reshape(B, L, H * dh)


def sliding_window_attention(q, k, v, kc, vc, sink):
    B, L, KV, G, dh = q.shape
    C = kc.shape[1]
    T = SWA_BLOCK
    nb = L // T
    scale = dh ** -0.5
    qb = q.reshape(B, nb, T, KV, G, dh)
    pad = jnp.zeros((B, T, KV, dh), k.dtype)
    kp = jnp.concatenate([pad, k, pad], axis=1)
    vp = jnp.concatenate([pad, v, pad], axis=1)
    rel = (jnp.arange(3 * T)[None, :] - T) - jnp.arange(T)[:, None]
    s_sink = jnp.broadcast_to(sink.reshape(1, KV, G, 1, 1).astype(jnp.float32), (B, KV, G, T, 1))

    def one_block(j):
        q_blk = lax.dynamic_index_in_dim(qb, j, axis=1, keepdims=False)
        k_blk = lax.dynamic_slice_in_dim(kp, j * T, 3 * T, axis=1)
        v_blk = lax.dynamic_slice_in_dim(vp, j * T, 3 * T, axis=1)
        kpos = (j - 1) * T + jnp.arange(3 * T)
        valid = (jnp.abs(rel) <= SWA_WINDOW) & ((kpos >= 0) & (kpos < L))[None, :]
        s_loc = jnp.einsum('bqkgd,bskd->bkgqs', q_blk, k_blk,
                           preferred_element_type=jnp.float32) * scale
        s_loc = jnp.where(valid, s_loc, NEG_INF)
        s_ctx = jnp.einsum('bqkgd,bckd->bkgqc', q_blk, kc,
                           preferred_element_type=jnp.float32) * scale
        s = jnp.concatenate([s_loc, s_ctx, s_sink], axis=-1)
        p = jax.nn.softmax(s, axis=-1).astype(v.dtype)
        return (jnp.einsum('bkgqs,bskd->bqkgd', p[..., :3 * T], v_blk)
                + jnp.einsum('bkgqc,bckd->bqkgd', p[..., 3 * T:3 * T + C], vc))

    out = lax.map(one_block, jnp.arange(nb))
    return jnp.transpose(out, (1, 0, 2, 3, 4, 5)).reshape(B, L, KV * G * dh)


def setup_inputs(seed: int = 0) -> dict:
    key = jax.random.key(seed)
    ks = jax.random.split(key, 24)
    f = jnp.float32
    d = D_MODEL
    kv_w = SWA_KV_HEADS * HEAD_DIM

    def nrm(k, shape, scale):
        return jax.random.normal(k, shape, f) * scale

    return {
        "x": nrm(ks[0], (BATCH, SEQ, d), 1.0),
        "c": nrm(ks[1], (BATCH, d), 1.0),
        "ctx": nrm(ks[2], (BATCH, CTX_LEN, d), 1.0),
        "c_ctx": nrm(ks[3], (d,), 1.0),
        "ada_w": nrm(ks[4], (DEPTH, d, 6 * d), 0.5 * d ** -0.5),
        "ada_b": nrm(ks[5], (DEPTH, 6 * d), 0.02),
        "g_mix": 1.0 + nrm(ks[6], (DEPTH, d), 0.02),
        "g_mlp": 1.0 + nrm(ks[7], (DEPTH, d), 0.02),
        "mlp_w1": nrm(ks[8], (DEPTH, d, D_FF), d ** -0.5),
        "mlp_w2": nrm(ks[9], (DEPTH, D_FF, d), D_FF ** -0.5),
        "na_wqkv": nrm(ks[10], (N_NA_LAYERS, d, 3 * d), d ** -0.5),
        "na_q_gain": 1.0 + nrm(ks[11], (N_NA_LAYERS, HEAD_DIM), 0.02),
        "na_k_gain": 1.0 + nrm(ks[12], (N_NA_LAYERS, HEAD_DIM), 0.02),
        "na_rpb": nrm(ks[13], (N_NA_LAYERS, N_HEADS, 2 * NA_KH - 1, 2 * NA_KW - 1), 0.1),
        "na_wo": nrm(ks[14], (N_NA_LAYERS, d, d), d ** -0.5),
        "swa_wqkv": nrm(ks[15], (N_SWA_LAYERS, d, d + 2 * kv_w), d ** -0.5),
        "swa_q_gain": 1.0 + nrm(ks[16], (N_SWA_LAYERS, HEAD_DIM), 0.02),
        "swa_k_gain": 1.0 + nrm(ks[17], (N_SWA_LAYERS, HEAD_DIM), 0.02),
        "swa_sink": nrm(ks[18], (N_SWA_LAYERS, N_HEADS), 0.5),
        "swa_wo": nrm(ks[19], (N_SWA_LAYERS, d, d), d ** -0.5),
    }


def reference(x, c, ctx, c_ctx, ada_w, ada_b, g_mix, g_mlp, mlp_w1, mlp_w2,
              na_wqkv, na_q_gain, na_k_gain, na_rpb, na_wo,
              swa_wqkv, swa_q_gain, swa_k_gain, swa_sink, swa_wo):
    B, L, D = x.shape
    C = ctx.shape[1]
    H, dh, KV, G = N_HEADS, HEAD_DIM, SWA_KV_HEADS, SWA_GROUP
    cos, sin = axial_rope_tables(L)
    silu_c = jax.nn.silu(c)
    silu_cc = jax.nn.silu(c_ctx)
    h_lat = x
    h_ctx = ctx
    for i in range(DEPTH):
        last = i == DEPTH - 1
        j = i // N_MIXERS
        mod_lat = (silu_c @ ada_w[i] + ada_b[i])[:, None, :]
        mod_ctx = (silu_cc @ ada_w[i] + ada_b[i])[None, None, :]
        sh1, sc1, ga1, sh2, sc2, ga2 = jnp.split(mod_lat, 6, axis=-1)
        csh1, csc1, cga1, csh2, csc2, cga2 = jnp.split(mod_ctx, 6, axis=-1)
        u_lat = modulate(rms_norm(h_lat, g_mix[i]), sh1, sc1)
        u_ctx = modulate(rms_norm(h_ctx, g_mix[i]), csh1, csc1)

        if i % N_MIXERS == 0:
            w = na_wqkv[j]
            ql, kl, vl = jnp.split((u_lat @ w).reshape(B, L, 3, H, dh), 3, axis=2)
            qc, kc, vc = jnp.split((u_ctx @ w).reshape(B, C, 3, H, dh), 3, axis=2)
            ql, kl, vl = ql[:, :, 0], kl[:, :, 0], vl[:, :, 0]
            qc, kc, vc = qc[:, :, 0], kc[:, :, 0], vc[:, :, 0]
            ql, qc = rms_norm(ql, na_q_gain[j]), rms_norm(qc, na_q_gain[j])
            kl, kc = rms_norm(kl, na_k_gain[j]), rms_norm(kc, na_k_gain[j])
            y_lat = neighborhood_attention(ql, kl, vl, kc, vc, na_rpb[j]) @ na_wo[j]
            if not last:
                y_ctx = context_attention(qc[:, :, :, None, :], kc, vc) @ na_wo[j]
        else:
            w = swa_wqkv[j]
            pl = u_lat @ w
            pc = u_ctx @ w
            kvw = KV * dh
            ql = pl[..., :D].reshape(B, L, H, dh)
            kl = pl[..., D:D + kvw].reshape(B, L, KV, dh)
            vl = pl[..., D + kvw:].reshape(B, L, KV, dh)
            qc = pc[..., :D].reshape(B, C, H, dh)
            kc = pc[..., D:D + kvw].reshape(B, C, KV, dh)
            vc = pc[..., D + kvw:].reshape(B, C, KV, dh)
            ql, qc = rms_norm(ql, swa_q_gain[j]), rms_norm(qc, swa_q_gain[j])
            kl, kc = rms_norm(kl, swa_k_gain[j]), rms_norm(kc, swa_k_gain[j])
            ql = apply_axial_rope(ql, cos, sin)
            kl = apply_axial_rope(kl, cos, sin)
            y_lat = sliding_window_attention(ql.reshape(B, L, KV, G, dh), kl, vl,
                                             kc, vc, swa_sink[j]) @ swa_wo[j]
            if not last:
                y_ctx = context_attention(qc.reshape(B, C, KV, G, dh), kc, vc,
                                          sink=swa_sink[j]) @ swa_wo[j]

        h_lat = h_lat + ga1 * y_lat
        h_lat = h_lat + ga2 * squared_relu_mlp(
            modulate(rms_norm(h_lat, g_mlp[i]), sh2, sc2), mlp_w1[i], mlp_w2[i])
        if not last:
            h_ctx = h_ctx + cga1 * y_ctx
            h_ctx = h_ctx + cga2 * squared_relu_mlp(
                modulate(rms_norm(h_ctx, g_mlp[i]), csh2, csc2), mlp_w1[i], mlp_w2[i])
    return h_lat
```

```python
import functools

import numpy as np
import jax
import jax.numpy as jnp
from jax import lax
from jax.experimental import pallas as pl
from jax.experimental.pallas import tpu as pltpu

GRID_W = 64
N_HEADS = 16
HEAD_DIM = 64
NA_KH = 8
NA_KW = 16
SWA_KV_HEADS = 4
SWA_GROUP = N_HEADS // SWA_KV_HEADS
SWA_WINDOW = 128
ROPE_BASE = 10000.0
NORM_EPS = 1e-6
NEG_INF = -1e30

F32 = jnp.float32
BF16 = jnp.bfloat16

V7X_VMEM_BYTES = 64 * 1024 * 1024
VMEM_LIMIT_BYTES = V7X_VMEM_BYTES - 8 * 1024 * 1024

HEAD_PAIR = 2 * HEAD_DIM
NA_Q_ROWS = 4
NA_K_ROWS = NA_Q_ROWS + NA_KH
NA_QB = NA_Q_ROWS * GRID_W
NA_KB = NA_K_ROWS * GRID_W
SWA_QB = 128
SWA_KB = SWA_QB + 2 * SWA_WINDOW


def _params(*semantics):
    return pltpu.CompilerParams(dimension_semantics=semantics, vmem_limit_bytes=VMEM_LIMIT_BYTES)


def _const_spec(shape):
    zeros = (0,) * len(shape)
    return pl.BlockSpec(shape, lambda *_: zeros, pipeline_mode=pl.Buffered(1))


def _mod_kernel(cond_ref, w_ref, b_ref, o_ref):
    s = jax.nn.silu(cond_ref[...])
    o_ref[0] = jnp.dot(s, w_ref[0], precision=lax.Precision.HIGHEST,
                       preferred_element_type=F32) + b_ref[0]


def _modulation(cond, ada_w, ada_b, tn=1536):
    depth, d, n = ada_w.shape
    rows = cond.shape[0]
    return pl.pallas_call(
        _mod_kernel,
        grid=(depth, n // tn),
        in_specs=[
            pl.BlockSpec((rows, d), lambda i, j: (0, 0)),
            pl.BlockSpec((1, d, tn), lambda i, j: (i, 0, j)),
            pl.BlockSpec((1, 1, tn), lambda i, j: (i, 0, j)),
        ],
        out_specs=pl.BlockSpec((1, rows, tn), lambda i, j: (i, 0, j)),
        out_shape=jax.ShapeDtypeStruct((depth, rows, n), F32),
        compiler_params=_params("arbitrary", "arbitrary"),
        name="adaln_modulation",
    )(cond, ada_w, ada_b.reshape(depth, 1, n))


def _rms_modulate(x, g, shift, scale):
    y = x * lax.rsqrt(jnp.mean(x * x, axis=-1, keepdims=True) + NORM_EPS) * g
    return y * (1.0 + scale) + shift


def _head_norm_t(a, gain_b, rope):
    n = a.shape[0] // HEAD_DIM
    tm = a.shape[1]
    a = a.reshape(n, HEAD_DIM, tm)
    a = a * lax.rsqrt(jnp.mean(a * a, axis=1, keepdims=True) + NORM_EPS) * gain_b[None]
    if rope is not None:
        cos_b, sin_b = rope
        q = HEAD_DIM // 4
        swapped = jnp.concatenate(
            [a[:, q:2 * q], a[:, 0:q], a[:, 3 * q:4 * q], a[:, 2 * q:3 * q]], axis=1)
        a = a * cos_b[None] + swapped * sin_b[None]
    return a.reshape(n * HEAD_DIM, tm)


def _qkv_kernel(*refs, d_model, kv_width, use_rope, chunk):
    if use_rope:
        h_ref, mod_ref, g_ref, wt_ref, gq_ref, gk_ref, cos_ref, sin_ref, q_ref, k_ref, v_ref = refs
        rope = (cos_ref[...], sin_ref[...])
    else:
        h_ref, mod_ref, g_ref, wt_ref, gq_ref, gk_ref, q_ref, k_ref, v_ref = refs
        rope = None
    mod = mod_ref[0]
    u = _rms_modulate(h_ref[0], g_ref[...], mod[0:1], mod[1:2]).astype(BF16)
    contract_last = (((1,), (1,)), ((), ()))

    def proj(r0):
        return lax.dot_general(wt_ref[r0:r0 + chunk, :], u, contract_last,
                               preferred_element_type=F32)

    gq = gq_ref[...]
    gk = gk_ref[...]
    for r0 in range(0, d_model, chunk):
        q_ref[0, r0:r0 + chunk, :] = _head_norm_t(proj(r0), gq, rope).astype(BF16)
    for r0 in range(0, kv_width, chunk):
        kt = _head_norm_t(proj(d_model + r0), gk, rope)
        k_ref[0, :, r0:r0 + chunk] = kt.T.astype(BF16)
    for r0 in range(0, kv_width, chunk):
        v_ref[0, r0:r0 + chunk, :] = proj(d_model + kv_width + r0).astype(BF16)


def _qkv_project(h, mod, g, wt, gq_b, gk_b, rope_tabs, *, kv_width, tm, chunk=256):
    b, t, d = h.shape
    use_rope = rope_tabs is not None
    in_specs = [
        pl.BlockSpec((1, tm, d), lambda i, j: (i, j, 0)),
        pl.BlockSpec((1, 6, d), lambda i, j: (i, 0, 0)),
        _const_spec((1, d)),
        _const_spec(wt.shape),
        _const_spec((HEAD_DIM, tm)),
        _const_spec((HEAD_DIM, tm)),
    ]
    args = [h, mod, g, wt, gq_b, gk_b]
    if use_rope:
        in_specs += [pl.BlockSpec((HEAD_DIM, tm), lambda i, j: (0, j))] * 2
        args += list(rope_tabs)
    return pl.pallas_call(
        functools.partial(_qkv_kernel, d_model=d, kv_width=kv_width, use_rope=use_rope, chunk=chunk),
        grid=(b, t // tm),
        in_specs=in_specs,
        out_specs=[
            pl.BlockSpec((1, d, tm), lambda i, j: (i, 0, j)),
            pl.BlockSpec((1, tm, kv_width), lambda i, j: (i, j, 0)),
            pl.BlockSpec((1, kv_width, tm), lambda i, j: (i, 0, j)),
        ],
        out_shape=[
            jax.ShapeDtypeStruct((b, d, t), BF16),
            jax.ShapeDtypeStruct((b, t, kv_width), BF16),
            jax.ShapeDtypeStruct((b, kv_width, t), BF16),
        ],
        compiler_params=_params("arbitrary", "arbitrary"),
        name="qkv_rope" if use_rope else "qkv",
    )(*args)


def _col_max(*parts):
    m = parts[0].max(axis=0, keepdims=True)
    for p in parts[1:]:
        m = jnp.maximum(m, p.max(axis=0, keepdims=True))
    return m


def _mask_rows_to_head(q_pair, head):
    row = lax.broadcasted_iota(jnp.int32, q_pair.shape, 0)
    return jnp.where((row // HEAD_DIM) == head, q_pair, jnp.zeros_like(q_pair))


def _na_kernel(q_ref, k_ref, v_ref, kc_ref, vc_ref, bias_ref, o_ref, *, n_blocks, grid_rows):
    kc = kc_ref[0]
    vc = vc_ref[0]

    def block(i, carry):
        start_row = jnp.clip(NA_Q_ROWS * i - NA_KH // 2, 0, grid_rows - NA_K_ROWS)
        k0 = pl.multiple_of(start_row * GRID_W, NA_QB)
        q0 = pl.multiple_of(i * NA_QB, NA_QB)
        variant = jnp.where(i == 0, 0, jnp.where(i == n_blocks - 1, 2, 1))
        qb = q_ref[0, :, pl.ds(q0, NA_QB)]
        kb = k_ref[0, pl.ds(k0, NA_KB), :]
        vb = v_ref[0, :, pl.ds(k0, NA_KB)]
        outs = []
        for hh in range(2):
            qm = _mask_rows_to_head(qb, hh)
            s_loc = jnp.dot(kb, qm, preferred_element_type=F32) + bias_ref[hh, variant]
            s_ctx = jnp.dot(kc, qm, preferred_element_type=F32)
            m = _col_max(s_loc, s_ctx)
            p_loc = jnp.exp(s_loc - m)
            p_ctx = jnp.exp(s_ctx - m)
            l = p_loc.sum(axis=0, keepdims=True) + p_ctx.sum(axis=0, keepdims=True)
            rows = slice(hh * HEAD_DIM, (hh + 1) * HEAD_DIM)
            o = (jnp.dot(vb[rows], p_loc.astype(BF16), preferred_element_type=F32)
                 + jnp.dot(vc[rows], p_ctx.astype(BF16), preferred_element_type=F32))
            outs.append(o / l)
        o_pair = jnp.concatenate(outs, axis=0)
        o_ref[0, pl.ds(q0, NA_QB), :] = o_pair.T.astype(BF16)
        return carry

    lax.fori_loop(0, n_blocks, block, 0)


def _na_attention(qt, k, vt, kc, vct, bias):
    b, d, l = qt.shape
    c = kc.shape[1]
    pairs = d // HEAD_PAIR
    grid_rows = l // GRID_W
    n_blocks = grid_rows // NA_Q_ROWS
    assert grid_rows >= NA_K_ROWS and n_blocks >= 3
    return pl.pallas_call(
        functools.partial(_na_kernel, n_blocks=n_blocks, grid_rows=grid_rows),
        grid=(pairs, b),
        in_specs=[
            pl.BlockSpec((1, HEAD_PAIR, l), lambda p, i: (i, p, 0)),
            pl.BlockSpec((1, l, HEAD_PAIR), lambda p, i: (i, 0, p)),
            pl.BlockSpec((1, HEAD_PAIR, l), lambda p, i: (i, p, 0)),
            pl.BlockSpec((1, c, HEAD_PAIR), lambda p, i: (i, 0, p)),
            pl.BlockSpec((1, HEAD_PAIR, c), lambda p, i: (i, p, 0)),
            pl.BlockSpec((2, 3, NA_KB, NA_QB), lambda p, i: (p, 0, 0, 0)),
        ],
        out_specs=pl.BlockSpec((1, l, HEAD_PAIR), lambda p, i: (i, 0, p)),
        out_shape=jax.ShapeDtypeStruct((b, l, d), BF16),
        compiler_params=_params("arbitrary", "arbitrary"),
        name="neighbourhood_attention",
    )(qt, k, vt, kc, vct, bias)


def _na_bias_tables(rpb):
    h = rpb.shape[0]
    w = GRID_W
    padded = jnp.pad(rpb, ((0, 0), (0, 0), (w - NA_KW, w - NA_KW)))
    flipped = padded[:, :, ::-1]
    tiles = jnp.stack([flipped[:, :, w - 1 - kc: 2 * w - 1 - kc] for kc in range(w)], axis=2)
    kc = np.arange(w)[:, None]
    qc = np.arange(w)[None, :]
    c0 = np.clip(qc - NA_KW // 2, 0, w - NA_KW)
    col_valid = (kc >= c0) & (kc < c0 + NA_KW)
    tiles = jnp.where(col_valid[None, None], tiles, NEG_INF)
    tiles = jnp.concatenate([tiles, jnp.full((h, 1, w, w), NEG_INF, tiles.dtype)], axis=1)
    masked = 2 * NA_KH - 1
    kr = np.arange(NA_K_ROWS)[:, None]
    qr = np.arange(NA_Q_ROWS)[None, :]
    half = NA_KH // 2
    variants = [
        (kr < NA_KH + 0 * qr, kr - qr + NA_KH - 1),
        ((kr >= qr) & (kr < qr + NA_KH), kr - qr + NA_KH - 1 - half),
        (kr >= NA_K_ROWS - NA_KH + 0 * qr, kr - qr + NA_KH - 1 - (NA_K_ROWS - NA_Q_ROWS)),
    ]
    idx = np.stack([np.where(valid, off, masked) for valid, off in variants])
    assert idx.min() >= 0 and idx.max() <= masked
    big = tiles[:, idx]
    big = jnp.transpose(big, (0, 1, 2, 4, 3, 5))
    return big.reshape(h, 3, NA_KB, NA_QB)


def _ctx_kernel(q_ref, k_ref, v_ref, o_ref, *, pairs):
    for p in range(pairs):
        cols = slice(p * HEAD_PAIR, (p + 1) * HEAD_PAIR)
        qb = q_ref[0, cols, :]
        kb = k_ref[0, :, cols]
        vb = v_ref[0, cols, :]
        outs = []
        for hh in range(2):
            s = jnp.dot(kb, _mask_rows_to_head(qb, hh), preferred_element_type=F32)
            pr = jnp.exp(s - s.max(axis=0, keepdims=True))
            l = pr.sum(axis=0, keepdims=True)
            rows = slice(hh * HEAD_DIM, (hh + 1) * HEAD_DIM)
            outs.append(jnp.dot(vb[rows], pr.astype(BF16), preferred_element_type=F32) / l)
        o_ref[0, :, cols] = jnp.concatenate(outs, axis=0).T.astype(BF16)


def _ctx_attention(qct, kc, vct):
    b, d, c = qct.shape
    return pl.pallas_call(
        functools.partial(_ctx_kernel, pairs=d // HEAD_PAIR),
        grid=(b,),
        in_specs=[
            pl.BlockSpec((1, d, c), lambda i: (i, 0, 0)),
            pl.BlockSpec((1, c, d), lambda i: (i, 0, 0)),
            pl.BlockSpec((1, d, c), lambda i: (i, 0, 0)),
        ],
        out_specs=pl.BlockSpec((1, c, d), lambda i: (i, 0, 0)),
        out_shape=jax.ShapeDtypeStruct((b, c, d), BF16),
        compiler_params=_params("arbitrary"),
        name="context_attention",
    )(qct, kc, vct)


def _swa_kernel(q_ref, k_ref, v_ref, kc_ref, vc_ref, mask_ref, sink_ref, o_ref, *, n_blocks, seq):
    kv_in_pair = pl.program_id(1) % 2
    kc = kc_ref[0]
    vc = vc_ref[0]
    sink = sink_ref[0]

    def block(j, carry):
        k0 = pl.multiple_of(jnp.clip(SWA_QB * j - SWA_WINDOW, 0, seq - SWA_KB), SWA_QB)
        q0 = pl.multiple_of(j * SWA_QB, SWA_QB)
        variant = jnp.where(j == 0, 0, jnp.where(j == n_blocks - 1, 2, 1))
        qg = jnp.concatenate(
            [q_ref[0, g * HEAD_DIM:(g + 1) * HEAD_DIM, pl.ds(q0, SWA_QB)] for g in range(SWA_GROUP)],
            axis=1)
        qm = _mask_rows_to_head(jnp.concatenate([qg, qg], axis=0), kv_in_pair)
        kb = k_ref[0, pl.ds(k0, SWA_KB), :]
        vb = v_ref[0, :, pl.ds(k0, SWA_KB)]
        mask = mask_ref[variant]
        s_loc = jnp.dot(kb, qm, preferred_element_type=F32) + jnp.concatenate([mask] * SWA_GROUP, axis=1)
        s_ctx = jnp.dot(kc, qm, preferred_element_type=F32)
        m = jnp.maximum(_col_max(s_loc, s_ctx), sink)
        p_loc = jnp.exp(s_loc - m)
        p_ctx = jnp.exp(s_ctx - m)
        l = p_loc.sum(axis=0, keepdims=True) + p_ctx.sum(axis=0, keepdims=True) + jnp.exp(sink - m)
        o = (jnp.dot(vb, p_loc.astype(BF16), preferred_element_type=F32)
             + jnp.dot(vc, p_ctx.astype(BF16), preferred_element_type=F32)) / l
        o_rows = jnp.concatenate([o[:, g * SWA_QB:(g + 1) * SWA_QB] for g in range(SWA_GROUP)], axis=0)
        o_ref[0, pl.ds(q0, SWA_QB), :] = o_rows.T.astype(BF16)
        return carry

    lax.fori_loop(0, n_blocks, block, 0)


def _swa_mask_tables():
    kr = np.arange(SWA_KB)[:, None]
    qc = np.arange(SWA_QB)[None, :]
    rel = [kr - qc, kr - SWA_WINDOW - qc, kr - (SWA_KB - SWA_QB) - qc]
    return np.stack([np.where(np.abs(r) <= SWA_WINDOW, 0.0, NEG_INF) for r in rel]).astype(np.float32)


def _swa_attention(qt, k, vt, kc, vct, sink_rows):
    b, d, l = qt.shape
    c = kc.shape[1]
    kv = k.shape[2] // HEAD_DIM
    gw = SWA_GROUP * HEAD_DIM
    n_blocks = l // SWA_QB
    assert l >= SWA_KB and n_blocks >= 3
    return pl.pallas_call(
        functools.partial(_swa_kernel, n_blocks=n_blocks, seq=l),
        grid=(b, kv),
        in_specs=[
            pl.BlockSpec((1, gw, l), lambda i, n: (i, n, 0)),
            pl.BlockSpec((1, l, HEAD_PAIR), lambda i, n: (i, 0, n // 2)),
            pl.BlockSpec((1, HEAD_DIM, l), lambda i, n: (i, n, 0)),
            pl.BlockSpec((1, c, HEAD_PAIR), lambda i, n: (i, 0, n // 2)),
            pl.BlockSpec((1, HEAD_DIM, c), lambda i, n: (i, n, 0)),
            _const_spec((3, SWA_KB, SWA_QB)),
            pl.BlockSpec((1, 1, SWA_GROUP * SWA_QB), lambda i, n: (n, 0, 0)),
        ],
        out_specs=pl.BlockSpec((1, l, gw), lambda i, n: (i, 0, n)),
        out_shape=jax.ShapeDtypeStruct((b, l, d), BF16),
        compiler_params=_params("arbitrary", "arbitrary"),
        name="sliding_window_attention",
    )(qt, k, vt, kc, vct, jnp.asarray(_swa_mask_tables()), sink_rows)


def _post_kernel(a_ref, h_ref, mod_ref, g_ref, wo_ref, w1_ref, w2_ref, o_ref, *, ff_chunk):
    mod = mod_ref[0]
    y = jnp.dot(a_ref[0], wo_ref[...], preferred_element_type=F32)
    h1 = h_ref[0] + mod[2:3] * y
    u = _rms_modulate(h1, g_ref[...], mod[3:4], mod[4:5]).astype(BF16)
    d_ff = w1_ref.shape[1]
    acc = jnp.zeros_like(h1)
    for c0 in range(0, d_ff, ff_chunk):
        t = jnp.dot(u, w1_ref[:, c0:c0 + ff_chunk], preferred_element_type=F32)
        r = jnp.square(jnp.maximum(t, 0.0)).astype(BF16)
        acc = acc + jnp.dot(r, w2_ref[c0:c0 + ff_chunk, :], preferred_element_type=F32)
    o_ref[0] = h1 + mod[5:6] * acc


def _post_block(a, h, mod, g, wo, w1, w2, *, tm, ff_chunk=1024):
    b, t, d = h.shape
    return pl.pallas_call(
        functools.partial(_post_kernel, ff_chunk=ff_chunk),
        grid=(b, t // tm),
        in_specs=[
            pl.BlockSpec((1, tm, d), lambda i, j: (i, j, 0)),
            pl.BlockSpec((1, tm, d), lambda i, j: (i, j, 0)),
            pl.BlockSpec((1, 6, d), lambda i, j: (i, 0, 0)),
            _const_spec((1, d)),
            _const_spec(wo.shape),
            _const_spec(w1.shape),
            _const_spec(w2.shape),
        ],
        out_specs=pl.BlockSpec((1, tm, d), lambda i, j: (i, j, 0)),
        out_shape=jax.ShapeDtypeStruct((b, t, d), F32),
        compiler_params=_params("arbitrary", "arbitrary"),
        name="outproj_mlp",
    )(a, h, mod, g, wo, w1, w2)


def _rope_tables_t(l):
    t = jnp.arange(l, dtype=jnp.int32)
    row = (t // GRID_W).astype(F32)
    col = (t % GRID_W).astype(F32)
    n_freq = HEAD_DIM // 4
    inv = ROPE_BASE ** (-jnp.arange(n_freq, dtype=F32) / n_freq)
    ang = jnp.stack([row[:, None] * inv, col[:, None] * inv], axis=1)
    cos = jnp.transpose(jnp.cos(ang), (1, 2, 0))
    sin = jnp.transpose(jnp.sin(ang), (1, 2, 0))
    cos_t = jnp.stack([cos, cos], axis=1).reshape(HEAD_DIM, l)
    sin_t = jnp.stack([-sin, sin], axis=1).reshape(HEAD_DIM, l)
    return cos_t, sin_t


def _gain_cols(gain, tm, scale=1.0):
    return jnp.broadcast_to((gain.astype(F32) * scale)[:, None], (HEAD_DIM, tm))


def kernel(x, c, ctx, c_ctx, ada_w, ada_b, g_mix, g_mlp, mlp_w1, mlp_w2, na_wqkv, na_q_gain, na_k_gain,
           na_rpb, na_wo, swa_wqkv, swa_q_gain, swa_k_gain, swa_sink, swa_wo):
    b, l, d = x.shape
    n_ctx = ctx.shape[1]
    tm = 512
    tmc = n_ctx
    scale = HEAD_DIM ** -0.5

    cond = jnp.concatenate([c, c_ctx[None], jnp.zeros((8 - b - 1, d), F32)], axis=0)
    mods = _modulation(cond, ada_w, ada_b)

    def mods_of(i):
        lat = mods[i, :b].reshape(b, 6, d)
        cx = jnp.broadcast_to(mods[i, b].reshape(1, 6, d), (b, 6, d))
        return lat, cx

    def row(v):
        return v.reshape(1, d)

    mod_lat, mod_ctx = mods_of(0)
    wt = na_wqkv[0].T.astype(BF16)
    gq, gk = na_q_gain[0], na_k_gain[0]
    qt, k, vt = _qkv_project(x, mod_lat, row(g_mix[0]), wt, _gain_cols(gq, tm, scale), _gain_cols(gk, tm),
                             None, kv_width=d, tm=tm)
    qct, kc, vct = _qkv_project(ctx, mod_ctx, row(g_mix[0]), wt, _gain_cols(gq, tmc, scale),
                                _gain_cols(gk, tmc), None, kv_width=d, tm=tmc)
    a_lat = _na_attention(qt, k, vt, kc, vct, _na_bias_tables(na_rpb[0]))
    a_ctx = _ctx_attention(qct, kc, vct)
    wo = na_wo[0].astype(BF16)
    w1 = mlp_w1[0].astype(BF16)
    w2 = mlp_w2[0].astype(BF16)
    h_lat = _post_block(a_lat, x, mod_lat, row(g_mlp[0]), wo, w1, w2, tm=tm)
    h_ctx = _post_block(a_ctx, ctx, mod_ctx, row(g_mlp[0]), wo, w1, w2, tm=tmc)

    mod_lat, mod_ctx = mods_of(1)
    kvw = SWA_KV_HEADS * HEAD_DIM
    wt = swa_wqkv[0].T.astype(BF16)
    gq, gk = swa_q_gain[0], swa_k_gain[0]
    cos_t, sin_t = _rope_tables_t(l)
    qt, k, vt = _qkv_project(h_lat, mod_lat, row(g_mix[1]), wt, _gain_cols(gq, tm, scale), _gain_cols(gk, tm),
                             (cos_t, sin_t), kv_width=kvw, tm=tm)
    _, kc, vct = _qkv_project(h_ctx, mod_ctx, row(g_mix[1]), wt, _gain_cols(gq, tmc, scale),
                              _gain_cols(gk, tmc), None, kv_width=kvw, tm=tmc)
    sink_rows = jnp.repeat(swa_sink[0].astype(F32), SWA_QB).reshape(SWA_KV_HEADS, 1, SWA_GROUP * SWA_QB)
    a_lat = _swa_attention(qt, k, vt, kc, vct, sink_rows)
    return _post_block(a_lat, h_lat, mod_lat, row(g_mlp[1]), swa_wo[0].astype(BF16),
                       mlp_w1[1].astype(BF16), mlp_w2[1].astype(BF16), tm=tm)
```

```python
import functools

import numpy as np
import jax
import jax.numpy as jnp
from jax import lax
from jax.experimental import pallas as pl
from jax.experimental.pallas import tpu as pltpu

GRID_W = 64
N_HEADS = 16
HEAD_DIM = 64
NA_KH = 8
NA_KW = 16
SWA_KV_HEADS = 4
SWA_GROUP = N_HEADS // SWA_KV_HEADS
SWA_WINDOW = 128
ROPE_BASE = 10000.0
NORM_EPS = 1e-6
NEG_INF = -1e30
LOG2E = float(np.log2(np.e))

F32 = jnp.float32
BF16 = jnp.bfloat16

V7X_VMEM_BYTES = 64 * 1024 * 1024
VMEM_LIMIT_BYTES = V7X_VMEM_BYTES - 8 * 1024 * 1024

HEAD_PAIR = 2 * HEAD_DIM
NA_Q_ROWS = 4
NA_K_ROWS = NA_Q_ROWS + NA_KH
NA_QB = NA_Q_ROWS * GRID_W
NA_KB = NA_K_ROWS * GRID_W
SWA_QB = 128
SWA_KB = SWA_QB + 2 * SWA_WINDOW


def _params(*semantics):
    return pltpu.CompilerParams(dimension_semantics=semantics, vmem_limit_bytes=VMEM_LIMIT_BYTES)


def _const_spec(shape):
    zeros = (0,) * len(shape)
    return pl.BlockSpec(shape, lambda *_: zeros, pipeline_mode=pl.Buffered(1))


def _mod_kernel(cond_ref, w_ref, b_ref, o_ref):
    s = jax.nn.silu(cond_ref[...])
    o_ref[0] = jnp.dot(s, w_ref[0], precision=lax.Precision.HIGHEST,
                       preferred_element_type=F32) + b_ref[0]


def _modulation(cond, ada_w, ada_b, tn=1536):
    depth, d, n = ada_w.shape
    rows = cond.shape[0]
    return pl.pallas_call(
        _mod_kernel,
        grid=(depth, n // tn),
        in_specs=[
            pl.BlockSpec((rows, d), lambda i, j: (0, 0)),
            pl.BlockSpec((1, d, tn), lambda i, j: (i, 0, j)),
            pl.BlockSpec((1, 1, tn), lambda i, j: (i, 0, j)),
        ],
        out_specs=pl.BlockSpec((1, rows, tn), lambda i, j: (i, 0, j)),
        out_shape=jax.ShapeDtypeStruct((depth, rows, n), F32),
        compiler_params=_params("arbitrary", "arbitrary"),
        name="adaln_modulation",
    )(cond, ada_w, ada_b.reshape(depth, 1, n))


def _rms_modulate(x, g, shift, scale):
    y = x * lax.rsqrt(jnp.mean(x * x, axis=-1, keepdims=True) + NORM_EPS) * g
    return y * (1.0 + scale) + shift


def _head_norm_t(a, gain_b, rope):
    n = a.shape[0] // HEAD_DIM
    tm = a.shape[1]
    a = a.reshape(n, HEAD_DIM, tm)
    a = a * lax.rsqrt(jnp.mean(a * a, axis=1, keepdims=True) + NORM_EPS) * gain_b[None]
    if rope is not None:
        cos_b, sin_b = rope
        q = HEAD_DIM // 4
        swapped = jnp.concatenate(
            [a[:, q:2 * q], a[:, 0:q], a[:, 3 * q:4 * q], a[:, 2 * q:3 * q]], axis=1)
        a = a * cos_b[None] + swapped * sin_b[None]
    return a.reshape(n * HEAD_DIM, tm)


def _qkv_kernel(*refs, d_model, kv_width, use_rope, chunk):
    if use_rope:
        h_ref, mod_ref, g_ref, wt_ref, gq_ref, gk_ref, cos_ref, sin_ref, q_ref, k_ref, v_ref = refs
        rope = (cos_ref[...], sin_ref[...])
    else:
        h_ref, mod_ref, g_ref, wt_ref, gq_ref, gk_ref, q_ref, k_ref, v_ref = refs
        rope = None
    mod = mod_ref[0]
    u = _rms_modulate(h_ref[0], g_ref[...], mod[0:1], mod[1:2]).astype(BF16)
    contract_last = (((1,), (1,)), ((), ()))

    def proj(r0):
        return lax.dot_general(wt_ref[r0:r0 + chunk, :], u, contract_last,
                               preferred_element_type=F32)

    gq = gq_ref[...]
    gk = gk_ref[...]
    for r0 in range(0, d_model, chunk):
        q_ref[0, r0:r0 + chunk, :] = _head_norm_t(proj(r0), gq, rope).astype(BF16)
    for r0 in range(0, kv_width, chunk):
        kt = _head_norm_t(proj(d_model + r0), gk, rope)
        k_ref[0, :, r0:r0 + chunk] = kt.T.astype(BF16)
    for r0 in range(0, kv_width, chunk):
        v_ref[0, r0:r0 + chunk, :] = proj(d_model + kv_width + r0).astype(BF16)


def _qkv_project(h, mod, g, wt, gq_b, gk_b, rope_tabs, *, kv_width, tm, chunk=256):
    b, t, d = h.shape
    use_rope = rope_tabs is not None
    in_specs = [
        pl.BlockSpec((1, tm, d), lambda i, j: (i, j, 0)),
        pl.BlockSpec((1, 6, d), lambda i, j: (i, 0, 0)),
        _const_spec((1, d)),
        _const_spec(wt.shape),
        _const_spec((HEAD_DIM, tm)),
        _const_spec((HEAD_DIM, tm)),
    ]
    args = [h, mod, g, wt, gq_b, gk_b]
    if use_rope:
        in_specs += [pl.BlockSpec((HEAD_DIM, tm), lambda i, j: (0, j))] * 2
        args += list(rope_tabs)
    return pl.pallas_call(
        functools.partial(_qkv_kernel, d_model=d, kv_width=kv_width, use_rope=use_rope, chunk=chunk),
        grid=(b, t // tm),
        in_specs=in_specs,
        out_specs=[
            pl.BlockSpec((1, d, tm), lambda i, j: (i, 0, j)),
            pl.BlockSpec((1, tm, kv_width), lambda i, j: (i, j, 0)),
            pl.BlockSpec((1, kv_width, tm), lambda i, j: (i, 0, j)),
        ],
        out_shape=[
            jax.ShapeDtypeStruct((b, d, t), BF16),
            jax.ShapeDtypeStruct((b, t, kv_width), BF16),
            jax.ShapeDtypeStruct((b, kv_width, t), BF16),
        ],
        compiler_params=_params("arbitrary", "arbitrary"),
        name="qkv_rope" if use_rope else "qkv",
    )(*args)


def _col_max(*parts):
    m = parts[0].max(axis=0, keepdims=True)
    for p in parts[1:]:
        m = jnp.maximum(m, p.max(axis=0, keepdims=True))
    return m


def _mask_rows_to_head(q_pair, head):
    row = lax.broadcasted_iota(jnp.int32, q_pair.shape, 0)
    return jnp.where((row // HEAD_DIM) == head, q_pair, jnp.zeros_like(q_pair))


def _na_kernel(q_ref, k_ref, v_ref, kc_ref, vc_ref, bias_ref, o_ref, sl_ref, sc_ref, m_ref, *,
               n_blocks, grid_rows, unroll):
    def window(i):
        start_row = jnp.clip(NA_Q_ROWS * i - NA_KH // 2, 0, grid_rows - NA_K_ROWS)
        return pl.multiple_of(start_row * GRID_W, NA_QB), pl.multiple_of(i * NA_QB, NA_QB)

    def scores(i, hh, slot):
        i = jnp.minimum(i, n_blocks - 1)
        k0, q0 = window(i)
        variant = jnp.where(i == 0, 0, jnp.where(i == n_blocks - 1, 2, 1))
        qm = _mask_rows_to_head(q_ref[0, :, pl.ds(q0, NA_QB)], hh)
        s_loc = jnp.dot(k_ref[0, pl.ds(k0, NA_KB), :], qm, preferred_element_type=F32) + bias_ref[hh, variant]
        s_ctx = jnp.dot(kc_ref[0], qm, preferred_element_type=F32)
        sl_ref[slot] = s_loc
        sc_ref[slot] = s_ctx
        m_ref[slot] = _col_max(s_loc, s_ctx)

    def attend(i, hh, slot):
        k0, _ = window(i)
        m = m_ref[slot]
        p_loc = jnp.exp2(sl_ref[slot] - m)
        p_ctx = jnp.exp2(sc_ref[slot] - m)
        l = p_loc.sum(axis=0, keepdims=True) + p_ctx.sum(axis=0, keepdims=True)
        rows = slice(hh * HEAD_DIM, (hh + 1) * HEAD_DIM)
        o = (jnp.dot(v_ref[0, rows, pl.ds(k0, NA_KB)], p_loc.astype(BF16), preferred_element_type=F32)
             + jnp.dot(vc_ref[0, rows, :], p_ctx.astype(BF16), preferred_element_type=F32))
        return o / l

    scores(0, 0, 0)

    def body(it, carry):
        base = it * unroll
        units = [(base + blk, hh) for blk in range(unroll) for hh in range(2)]
        units.append((base + unroll, 0))
        pair = []
        for n, (i, hh) in enumerate(units[:-1]):
            scores(units[n + 1][0], units[n + 1][1], (n + 1) % 2)
            pair.append(attend(i, hh, n % 2))
            if hh == 1:
                q0 = pl.multiple_of(i * NA_QB, NA_QB)
                o_ref[0, pl.ds(q0, NA_QB), :] = jnp.concatenate(pair, axis=0).T.astype(BF16)
                pair = []
        return carry

    lax.fori_loop(0, n_blocks // unroll, body, 0)


def _na_attention(qt, k, vt, kc, vct, bias, unroll=2):
    b, d, l = qt.shape
    c = kc.shape[1]
    pairs = d // HEAD_PAIR
    grid_rows = l // GRID_W
    n_blocks = grid_rows // NA_Q_ROWS
    assert grid_rows >= NA_K_ROWS and n_blocks >= 3 and n_blocks % unroll == 0
    return pl.pallas_call(
        functools.partial(_na_kernel, n_blocks=n_blocks, grid_rows=grid_rows, unroll=unroll),
        grid=(pairs, b),
        in_specs=[
            pl.BlockSpec((1, HEAD_PAIR, l), lambda p, i: (i, p, 0)),
            pl.BlockSpec((1, l, HEAD_PAIR), lambda p, i: (i, 0, p)),
            pl.BlockSpec((1, HEAD_PAIR, l), lambda p, i: (i, p, 0)),
            pl.BlockSpec((1, c, HEAD_PAIR), lambda p, i: (i, 0, p)),
            pl.BlockSpec((1, HEAD_PAIR, c), lambda p, i: (i, p, 0)),
            pl.BlockSpec((2, 3, NA_KB, NA_QB), lambda p, i: (p, 0, 0, 0)),
        ],
        out_specs=pl.BlockSpec((1, l, HEAD_PAIR), lambda p, i: (i, 0, p)),
        out_shape=jax.ShapeDtypeStruct((b, l, d), BF16),
        scratch_shapes=[
            pltpu.VMEM((2, NA_KB, NA_QB), F32),
            pltpu.VMEM((2, c, NA_QB), F32),
            pltpu.VMEM((2, 1, NA_QB), F32),
        ],
        compiler_params=_params("arbitrary", "arbitrary"),
        name="neighbourhood_attention",
    )(qt, k, vt, kc, vct, bias)


def _na_bias_tables(rpb):
    h = rpb.shape[0]
    w = GRID_W
    padded = jnp.pad(rpb, ((0, 0), (0, 0), (w - NA_KW, w - NA_KW)))
    flipped = padded[:, :, ::-1]
    tiles = jnp.stack([flipped[:, :, w - 1 - kc: 2 * w - 1 - kc] for kc in range(w)], axis=2)
    kc = np.arange(w)[:, None]
    qc = np.arange(w)[None, :]
    c0 = np.clip(qc - NA_KW // 2, 0, w - NA_KW)
    col_valid = (kc >= c0) & (kc < c0 + NA_KW)
    tiles = jnp.where(col_valid[None, None], tiles, NEG_INF)
    tiles = jnp.concatenate([tiles, jnp.full((h, 1, w, w), NEG_INF, tiles.dtype)], axis=1)
    masked = 2 * NA_KH - 1
    kr = np.arange(NA_K_ROWS)[:, None]
    qr = np.arange(NA_Q_ROWS)[None, :]
    half = NA_KH // 2
    variants = [
        (kr < NA_KH + 0 * qr, kr - qr + NA_KH - 1),
        ((kr >= qr) & (kr < qr + NA_KH), kr - qr + NA_KH - 1 - half),
        (kr >= NA_K_ROWS - NA_KH + 0 * qr, kr - qr + NA_KH - 1 - (NA_K_ROWS - NA_Q_ROWS)),
    ]
    idx = np.stack([np.where(valid, off, masked) for valid, off in variants])
    assert idx.min() >= 0 and idx.max() <= masked
    big = jnp.concatenate([tiles[:, idx[:, :, qr]] for qr in range(NA_Q_ROWS)], axis=-1)
    return big.reshape(h, 3, NA_KB, NA_QB)


def _ctx_kernel(q_ref, k_ref, v_ref, o_ref, *, pairs):
    for p in range(pairs):
        cols = slice(p * HEAD_PAIR, (p + 1) * HEAD_PAIR)
        qb = q_ref[0, cols, :]
        kb = k_ref[0, :, cols]
        vb = v_ref[0, cols, :]
        outs = []
        for hh in range(2):
            s = jnp.dot(kb, _mask_rows_to_head(qb, hh), preferred_element_type=F32)
            pr = jnp.exp2(s - s.max(axis=0, keepdims=True))
            l = pr.sum(axis=0, keepdims=True)
            rows = slice(hh * HEAD_DIM, (hh + 1) * HEAD_DIM)
            outs.append(jnp.dot(vb[rows], pr.astype(BF16), preferred_element_type=F32) / l)
        o_ref[0, :, cols] = jnp.concatenate(outs, axis=0).T.astype(BF16)


def _ctx_attention(qct, kc, vct):
    b, d, c = qct.shape
    return pl.pallas_call(
        functools.partial(_ctx_kernel, pairs=d // HEAD_PAIR),
        grid=(b,),
        in_specs=[
            pl.BlockSpec((1, d, c), lambda i: (i, 0, 0)),
            pl.BlockSpec((1, c, d), lambda i: (i, 0, 0)),
            pl.BlockSpec((1, d, c), lambda i: (i, 0, 0)),
        ],
        out_specs=pl.BlockSpec((1, c, d), lambda i: (i, 0, 0)),
        out_shape=jax.ShapeDtypeStruct((b, c, d), BF16),
        compiler_params=_params("arbitrary"),
        name="context_attention",
    )(qct, kc, vct)


def _swa_kernel(q_ref, k_ref, v_ref, kc_ref, vc_ref, mask_ref, sink_ref, o_ref, sl_ref, sc_ref, m_ref, *,
                n_blocks, seq, unroll):
    kv_in_pair = pl.program_id(1) % 2

    def window(j):
        k0 = pl.multiple_of(jnp.clip(SWA_QB * j - SWA_WINDOW, 0, seq - SWA_KB), SWA_QB)
        return k0, pl.multiple_of(j * SWA_QB, SWA_QB)

    def scores(j, slot):
        j = jnp.minimum(j, n_blocks - 1)
        k0, q0 = window(j)
        variant = jnp.where(j == 0, 0, jnp.where(j == n_blocks - 1, 2, 1))
        qg = jnp.concatenate(
            [q_ref[0, g * HEAD_DIM:(g + 1) * HEAD_DIM, pl.ds(q0, SWA_QB)] for g in range(SWA_GROUP)],
            axis=1)
        qm = _mask_rows_to_head(jnp.concatenate([qg, qg], axis=0), kv_in_pair)
        mask = mask_ref[variant]
        s_loc = (jnp.dot(k_ref[0, pl.ds(k0, SWA_KB), :], qm, preferred_element_type=F32)
                 + jnp.concatenate([mask] * SWA_GROUP, axis=1))
        s_ctx = jnp.dot(kc_ref[0], qm, preferred_element_type=F32)
        sl_ref[slot] = s_loc
        sc_ref[slot] = s_ctx
        m_ref[slot] = jnp.maximum(_col_max(s_loc, s_ctx), sink_ref[0])

    def attend(j, slot):
        k0, q0 = window(j)
        m = m_ref[slot]
        p_loc = jnp.exp2(sl_ref[slot] - m)
        p_ctx = jnp.exp2(sc_ref[slot] - m)
        l = p_loc.sum(axis=0, keepdims=True) + p_ctx.sum(axis=0, keepdims=True) + jnp.exp2(sink_ref[0] - m)
        o = (jnp.dot(v_ref[0, :, pl.ds(k0, SWA_KB)], p_loc.astype(BF16), preferred_element_type=F32)
             + jnp.dot(vc_ref[0], p_ctx.astype(BF16), preferred_element_type=F32)) / l
        o_rows = jnp.concatenate([o[:, g * SWA_QB:(g + 1) * SWA_QB] for g in range(SWA_GROUP)], axis=0)
        o_ref[0, pl.ds(q0, SWA_QB), :] = o_rows.T.astype(BF16)

    scores(0, 0)

    def body(it, carry):
        base = it * unroll
        for n in range(unroll):
            scores(base + n + 1, (n + 1) % 2)
            attend(base + n, n % 2)
        return carry

    lax.fori_loop(0, n_blocks // unroll, body, 0)


def _swa_mask_tables():
    kr = np.arange(SWA_KB)[:, None]
    qc = np.arange(SWA_QB)[None, :]
    rel = [kr - qc, kr - SWA_WINDOW - qc, kr - (SWA_KB - SWA_QB) - qc]
    return np.stack([np.where(np.abs(r) <= SWA_WINDOW, 0.0, NEG_INF) for r in rel]).astype(np.float32)


def _swa_attention(qt, k, vt, kc, vct, sink_rows, unroll=2):
    b, d, l = qt.shape
    c = kc.shape[1]
    kv = k.shape[2] // HEAD_DIM
    gw = SWA_GROUP * HEAD_DIM
    n_blocks = l // SWA_QB
    assert l >= SWA_KB and n_blocks >= 3 and unroll % 2 == 0 and n_blocks % unroll == 0
    return pl.pallas_call(
        functools.partial(_swa_kernel, n_blocks=n_blocks, seq=l, unroll=unroll),
        grid=(b, kv),
        in_specs=[
            pl.BlockSpec((1, gw, l), lambda i, n: (i, n, 0)),
            pl.BlockSpec((1, l, HEAD_PAIR), lambda i, n: (i, 0, n // 2)),
            pl.BlockSpec((1, HEAD_DIM, l), lambda i, n: (i, n, 0)),
            pl.BlockSpec((1, c, HEAD_PAIR), lambda i, n: (i, 0, n // 2)),
            pl.BlockSpec((1, HEAD_DIM, c), lambda i, n: (i, n, 0)),
            _const_spec((3, SWA_KB, SWA_QB)),
            pl.BlockSpec((1, 1, SWA_GROUP * SWA_QB), lambda i, n: (n, 0, 0)),
        ],
        out_specs=pl.BlockSpec((1, l, gw), lambda i, n: (i, 0, n)),
        out_shape=jax.ShapeDtypeStruct((b, l, d), BF16),
        scratch_shapes=[
            pltpu.VMEM((2, SWA_KB, SWA_GROUP * SWA_QB), F32),
            pltpu.VMEM((2, c, SWA_GROUP * SWA_QB), F32),
            pltpu.VMEM((2, 1, SWA_GROUP * SWA_QB), F32),
        ],
        compiler_params=_params("arbitrary", "arbitrary"),
        name="sliding_window_attention",
    )(qt, k, vt, kc, vct, jnp.asarray(_swa_mask_tables()), sink_rows)


def _post_kernel(a_ref, h_ref, mod_ref, g_ref, wo_ref, w1_ref, w2_ref, o_ref, *, ff_chunk):
    mod = mod_ref[0]
    y = jnp.dot(a_ref[0], wo_ref[...], preferred_element_type=F32)
    h1 = h_ref[0] + mod[2:3] * y
    u = _rms_modulate(h1, g_ref[...], mod[3:4], mod[4:5]).astype(BF16)
    d_ff = w1_ref.shape[1]
    acc = jnp.zeros_like(h1)
    for c0 in range(0, d_ff, ff_chunk):
        t = jnp.dot(u, w1_ref[:, c0:c0 + ff_chunk], preferred_element_type=F32)
        r = jnp.square(jnp.maximum(t, 0.0)).astype(BF16)
        acc = acc + jnp.dot(r, w2_ref[c0:c0 + ff_chunk, :], preferred_element_type=F32)
    o_ref[0] = h1 + mod[5:6] * acc


def _post_block(a, h, mod, g, wo, w1, w2, *, tm, ff_chunk=1024):
    b, t, d = h.shape
    return pl.pallas_call(
        functools.partial(_post_kernel, ff_chunk=ff_chunk),
        grid=(b, t // tm),
        in_specs=[
            pl.BlockSpec((1, tm, d), lambda i, j: (i, j, 0)),
            pl.BlockSpec((1, tm, d), lambda i, j: (i, j, 0)),
            pl.BlockSpec((1, 6, d), lambda i, j: (i, 0, 0)),
            _const_spec((1, d)),
            _const_spec(wo.shape),
            _const_spec(w1.shape),
            _const_spec(w2.shape),
        ],
        out_specs=pl.BlockSpec((1, tm, d), lambda i, j: (i, j, 0)),
        out_shape=jax.ShapeDtypeStruct((b, t, d), F32),
        compiler_params=_params("arbitrary", "arbitrary"),
        name="outproj_mlp",
    )(a, h, mod, g, wo, w1, w2)


def _rope_tables_t(l):
    t = jnp.arange(l, dtype=jnp.int32)
    row = (t // GRID_W).astype(F32)
    col = (t % GRID_W).astype(F32)
    n_freq = HEAD_DIM // 4
    inv = ROPE_BASE ** (-jnp.arange(n_freq, dtype=F32) / n_freq)
    ang = jnp.stack([row[:, None] * inv, col[:, None] * inv], axis=1)
    cos = jnp.transpose(jnp.cos(ang), (1, 2, 0))
    sin = jnp.transpose(jnp.sin(ang), (1, 2, 0))
    cos_t = jnp.stack([cos, cos], axis=1).reshape(HEAD_DIM, l)
    sin_t = jnp.stack([-sin, sin], axis=1).reshape(HEAD_DIM, l)
    return cos_t, sin_t


def _gain_cols(gain, tm, scale=1.0):
    return jnp.broadcast_to((gain.astype(F32) * scale)[:, None], (HEAD_DIM, tm))


def kernel(x, c, ctx, c_ctx, ada_w, ada_b, g_mix, g_mlp, mlp_w1, mlp_w2, na_wqkv, na_q_gain, na_k_gain,
           na_rpb, na_wo, swa_wqkv, swa_q_gain, swa_k_gain, swa_sink, swa_wo):
    b, l, d = x.shape
    n_ctx = ctx.shape[1]
    tm = 512
    tmc = n_ctx
    scale = HEAD_DIM ** -0.5 * LOG2E

    cond =jnp.concatenate([c, c_ctx[None], jnp.zeros((8 - b - 1, d), F32)], axis=0)
    mods = _modulation(cond, ada_w, ada_b)

    def mods_of(i):
        lat = mods[i, :b].reshape(b, 6, d)
        cx = jnp.broadcast_to(mods[i, b].reshape(1, 6, d), (b, 6, d))
        return lat, cx

    def row(v):
        return v.reshape(1, d)

    mod_lat, mod_ctx = mods_of(0)
    wt = na_wqkv[0].T.astype(BF16)
    gq, gk = na_q_gain[0], na_k_gain[0]
    qt, k, vt = _qkv_project(x, mod_lat, row(g_mix[0]), wt, _gain_cols(gq, tm, scale), _gain_cols(gk, tm),
                             None, kv_width=d, tm=tm)
    qct, kc, vct = _qkv_project(ctx, mod_ctx, row(g_mix[0]), wt, _gain_cols(gq, tmc, scale),
                                _gain_cols(gk, tmc), None, kv_width=d, tm=tmc)
    a_lat = _na_attention(qt, k, vt, kc, vct, _na_bias_tables(na_rpb[0] * LOG2E))
    a_ctx = _ctx_attention(qct, kc, vct)
    wo = na_wo[0].astype(BF16)
    w1 = mlp_w1[0].astype(BF16)
    w2 = mlp_w2[0].astype(BF16)
    h_lat = _post_block(a_lat, x, mod_lat, row(g_mlp[0]), wo, w1, w2, tm=tm)
    h_ctx = _post_block(a_ctx, ctx, mod_ctx, row(g_mlp[0]), wo, w1, w2, tm=tmc)

    mod_lat, mod_ctx = mods_of(1)
    kvw = SWA_KV_HEADS * HEAD_DIM
    wt = swa_wqkv[0].T.astype(BF16)
    gq, gk = swa_q_gain[0], swa_k_gain[0]
    cos_t, sin_t = _rope_tables_t(l)
    qt, k, vt = _qkv_project(h_lat, mod_lat, row(g_mix[1]), wt, _gain_cols(gq, tm, scale), _gain_cols(gk, tm),
                             (cos_t, sin_t), kv_width=kvw, tm=tm)
    _, kc, vct = _qkv_project(h_ctx, mod_ctx, row(g_mix[1]), wt, _gain_cols(gq, tmc, scale),
                              _gain_cols(gk, tmc), None, kv_width=kvw, tm=tmc)
    sink_rows = jnp.repeat(swa_sink[0].astype(F32) * LOG2E, SWA_QB).reshape(SWA_KV_HEADS, 1, SWA_GROUP * SWA_QB)
    a_lat = _swa_attention(qt, k, vt, kc, vct, sink_rows)
    return _post_block(a_lat, h_lat, mod_lat, row(g_mlp[1]), swa_wo[0].astype(BF16),
                       mlp_w1[1].astype(BF16), mlp_w2[1].astype(BF16), tm=tm)
```

```python
import functools

import numpy as np
import jax
import jax.numpy as jnp
from jax import lax
from jax.experimental import pallas as pl
from jax.experimental.pallas import tpu as pltpu

GRID_W = 64
N_HEADS = 16
HEAD_DIM = 64
NA_KH = 8
NA_KW = 16
SWA_KV_HEADS = 4
SWA_GROUP = N_HEADS // SWA_KV_HEADS
SWA_WINDOW = 128
ROPE_BASE = 10000.0
NORM_EPS = 1e-6
NEG_INF = -1e30
LOG2E = float(np.log2(np.e))

F32 = jnp.float32
BF16 = jnp.bfloat16

V7X_VMEM_BYTES = 64 * 1024 * 1024
VMEM_LIMIT_BYTES = V7X_VMEM_BYTES - 8 * 1024 * 1024

BF16_SUBLANES = 16
HEAD_PAIR = 2 * HEAD_DIM
NA_Q_ROWS = 4
NA_K_ROWS = NA_Q_ROWS + NA_KH
NA_QB = NA_Q_ROWS * GRID_W
NA_KB = NA_K_ROWS * GRID_W
SWA_QB = 128
SWA_KB = SWA_QB + 2 * SWA_WINDOW
PIPE_AHEAD = 2
PIPE_SLOTS = 4


def _params(*semantics):
    return pltpu.CompilerParams(dimension_semantics=semantics, vmem_limit_bytes=VMEM_LIMIT_BYTES)


def _const_spec(shape):
    zeros = (0,) * len(shape)
    return pl.BlockSpec(shape, lambda *_: zeros, pipeline_mode=pl.Buffered(1))


def _mod_kernel(cond_ref, w_ref, b_ref, o_ref):
    s = jax.nn.silu(cond_ref[...])
    o_ref[0] = jnp.dot(s, w_ref[0], precision=lax.Precision.HIGHEST,
                       preferred_element_type=F32) + b_ref[0]


def _modulation(cond, ada_w, ada_b, tn=1536):
    depth, d, n = ada_w.shape
    rows = cond.shape[0]
    return pl.pallas_call(
        _mod_kernel,
        grid=(depth, n // tn),
        in_specs=[
            pl.BlockSpec((rows, d), lambda i, j: (0, 0)),
            pl.BlockSpec((1, d, tn), lambda i, j: (i, 0, j)),
            pl.BlockSpec((1, 1, tn), lambda i, j: (i, 0, j)),
        ],
        out_specs=pl.BlockSpec((1, rows, tn), lambda i, j: (i, 0, j)),
        out_shape=jax.ShapeDtypeStruct((depth, rows, n), F32),
        compiler_params=_params("arbitrary", "arbitrary"),
        name="adaln_modulation",
    )(cond, ada_w, ada_b.reshape(depth, 1, n))


def _rms_modulate(x, g, shift, scale):
    y = x * lax.rsqrt(jnp.mean(x * x, axis=-1, keepdims=True) + NORM_EPS) * g
    return y * (1.0 + scale) + shift


def _head_norm_t(a, gain_b, rope):
    n = a.shape[0] // HEAD_DIM
    tm = a.shape[1]
    a = a.reshape(n, HEAD_DIM, tm)
    a = a * lax.rsqrt(jnp.mean(a * a, axis=1, keepdims=True) + NORM_EPS) * gain_b[None]
    if rope is not None:
        cos_b, sin_b = rope
        q = HEAD_DIM // 4
        swapped = jnp.concatenate(
            [a[:, q:2 * q], a[:, 0:q], a[:, 3 * q:4 * q], a[:, 2 * q:3 * q]], axis=1)
        a = a * cos_b[None] + swapped * sin_b[None]
    return a.reshape(n * HEAD_DIM, tm)


def _qkv_kernel(*refs, d_model, kv_width, use_rope, chunk):
    if use_rope:
        h_ref, mod_ref, g_ref, wt_ref, gq_ref, gk_ref, cos_ref, sin_ref, q_ref, k_ref, v_ref = refs
        rope = (cos_ref[...], sin_ref[...])
    else:
        h_ref, mod_ref, g_ref, wt_ref, gq_ref, gk_ref, q_ref, k_ref, v_ref = refs
        rope = None
    mod = mod_ref[0]
    u = _rms_modulate(h_ref[0], g_ref[...], mod[0:1], mod[1:2]).astype(BF16)
    contract_last = (((1,), (1,)), ((), ()))

    def proj(r0):
        return lax.dot_general(wt_ref[r0:r0 + chunk, :], u, contract_last,
                               preferred_element_type=F32)

    gq = gq_ref[...]
    gk = gk_ref[...]
    for r0 in range(0, d_model, chunk):
        q_ref[0, r0:r0 + chunk, :] = _head_norm_t(proj(r0), gq, rope).astype(BF16)
    for r0 in range(0, kv_width, chunk):
        kt = _head_norm_t(proj(d_model + r0), gk, rope)
        k_ref[0, :, r0:r0 + chunk] = kt.T.astype(BF16)
    for r0 in range(0, kv_width, chunk):
        v_ref[0, r0:r0 + chunk, :] = proj(d_model + kv_width + r0).astype(BF16)


def _qkv_project(h, mod, g, wt, gq_b, gk_b, rope_tabs, *, kv_width, tm, chunk=256):
    b, t, d = h.shape
    use_rope = rope_tabs is not None
    in_specs = [
        pl.BlockSpec((1, tm, d), lambda i, j: (i, j, 0)),
        pl.BlockSpec((1, 6, d), lambda i, j: (i, 0, 0)),
        _const_spec((1, d)),
        _const_spec(wt.shape),
        _const_spec((HEAD_DIM, tm)),
        _const_spec((HEAD_DIM, tm)),
    ]
    args = [h, mod, g, wt, gq_b, gk_b]
    if use_rope:
        in_specs += [pl.BlockSpec((HEAD_DIM, tm), lambda i, j: (0, j))] * 2
        args += list(rope_tabs)
    return pl.pallas_call(
        functools.partial(_qkv_kernel, d_model=d, kv_width=kv_width, use_rope=use_rope, chunk=chunk),
        grid=(b, t // tm),
        in_specs=in_specs,
        out_specs=[
            pl.BlockSpec((1, d, tm), lambda i, j: (i, 0, j)),
            pl.BlockSpec((1, tm, kv_width), lambda i, j: (i, j, 0)),
            pl.BlockSpec((1, kv_width, tm), lambda i, j: (i, 0, j)),
        ],
        out_shape=[
            jax.ShapeDtypeStruct((b, d, t), BF16),
            jax.ShapeDtypeStruct((b, t, kv_width), BF16),
            jax.ShapeDtypeStruct((b, kv_width, t), BF16),
        ],
        compiler_params=_params("arbitrary", "arbitrary"),
        name="qkv_rope" if use_rope else "qkv",
    )(*args)


def _col_max(*parts):
    m = parts[0].max(axis=0, keepdims=True)
    for p in parts[1:]:
        m = jnp.maximum(m, p.max(axis=0, keepdims=True))
    return m


def _mask_rows_to_head(q_pair, head):
    row = lax.broadcasted_iota(jnp.int32, q_pair.shape, 0)
    return jnp.where((row // HEAD_DIM) == head, q_pair, jnp.zeros_like(q_pair))


def _na_build_bias(rows_ref, bias_ref):
    w = GRID_W
    kc = lax.broadcasted_iota(jnp.int32, (w, 2 * w), 0)
    qc = lax.broadcasted_iota(jnp.int32, (w, 2 * w), 1) % w
    c0 = jnp.clip(qc - NA_KW // 2, 0, w - NA_KW)
    col_valid = (kc >= c0) & (kc < c0 + NA_KW)
    plan, combos = _na_bias_plan()
    for hh in range(2):
        tiles = []
        for n in range(len(combos)):
            row = jnp.broadcast_to(rows_ref[hh, n:n + 1, :], (w, 2 * w))
            tiles.append(jnp.where(col_valid, pltpu.roll(row, 0, 1, stride=1, stride_axis=0), NEG_INF))
        for v in range(3):
            for kr in range(NA_K_ROWS):
                for half in range(NA_Q_ROWS // 2):
                    bias_ref[hh, v, kr * w:(kr + 1) * w, half * 2 * w:(half + 1) * 2 * w] = tiles[plan[v, kr, half]]


def _na_kernel(q_ref, k_ref, v_ref, kc_ref, vc_ref, rows_ref, o_ref, bias_ref, sl_ref, sc_ref, m_ref, *,
               n_blocks, grid_rows, unroll):
    @pl.when(pl.program_id(1) == 0)
    def _():
        _na_build_bias(rows_ref, bias_ref)

    ones_loc = jnp.ones((BF16_SUBLANES, NA_KB), BF16)
    ones_ctx = jnp.ones((BF16_SUBLANES, kc_ref.shape[1]), BF16)

    def window(i):
        start_row = jnp.clip(NA_Q_ROWS * i - NA_KH // 2, 0, grid_rows - NA_K_ROWS)
        return pl.multiple_of(start_row * GRID_W, NA_QB), pl.multiple_of(i * NA_QB, NA_QB)

    def scores(i, hh, slot):
        i = jnp.minimum(i, n_blocks - 1)
        k0, q0 = window(i)
        variant = jnp.where(i == 0, 0, jnp.where(i == n_blocks - 1, 2, 1))
        qm = _mask_rows_to_head(q_ref[0, :, pl.ds(q0, NA_QB)], hh)
        s_loc = jnp.dot(k_ref[0, pl.ds(k0, NA_KB), :], qm, preferred_element_type=F32) + bias_ref[hh, variant]
        s_ctx = jnp.dot(kc_ref[0], qm, preferred_element_type=F32)
        sl_ref[slot] = s_loc
        sc_ref[slot] = s_ctx
        m_ref[slot] = _col_max(s_loc, s_ctx)

    def attend(i, hh, slot):
        k0, _ = window(i)
        m = m_ref[slot]
        p_loc = jnp.exp2(sl_ref[slot] - m).astype(BF16)
        p_ctx = jnp.exp2(sc_ref[slot] - m).astype(BF16)
        rows = slice(hh * HEAD_DIM, (hh + 1) * HEAD_DIM)
        v_loc = jnp.concatenate([v_ref[0, rows, pl.ds(k0, NA_KB)], ones_loc], axis=0)
        v_ctx = jnp.concatenate([vc_ref[0, rows, :], ones_ctx], axis=0)
        o = (jnp.dot(v_loc, p_loc, preferred_element_type=F32)
             + jnp.dot(v_ctx, p_ctx, preferred_element_type=F32))
        return o[:HEAD_DIM] / o[HEAD_DIM:HEAD_DIM + 1]

    for n in range(PIPE_AHEAD):
        scores(n // 2, n % 2, n)

    def body(it, carry):
        base = it * unroll
        units = [(base + blk, hh) for blk in range(unroll + (PIPE_AHEAD + 1) // 2) for hh in range(2)]
        pair = []
        for n, (i, hh) in enumerate(units[:2 * unroll]):
            ahead = units[n + PIPE_AHEAD]
            scores(ahead[0], ahead[1], (n + PIPE_AHEAD) % PIPE_SLOTS)
            pair.append(attend(i, hh, n % PIPE_SLOTS))
            if hh == 1:
                q0 = pl.multiple_of(i * NA_QB, NA_QB)
                o_ref[0, pl.ds(q0, NA_QB), :] = jnp.concatenate(pair, axis=0).T.astype(BF16)
                pair = []
        return carry

    lax.fori_loop(0, n_blocks // unroll, body, 0)


def _na_attention(qt, k, vt, kc, vct, bias_rows, unroll=8):
    b, d, l = qt.shape
    c = kc.shape[1]
    pairs = d // HEAD_PAIR
    grid_rows = l // GRID_W
    n_blocks = grid_rows // NA_Q_ROWS
    n_combos = bias_rows.shape[1]
    assert grid_rows >= NA_K_ROWS and n_blocks >= 3 and n_blocks % unroll == 0
    assert (2 * unroll) % PIPE_SLOTS == 0 and PIPE_SLOTS > PIPE_AHEAD
    return pl.pallas_call(
        functools.partial(_na_kernel, n_blocks=n_blocks, grid_rows=grid_rows, unroll=unroll),
        grid=(pairs, b),
        in_specs=[
            pl.BlockSpec((1, HEAD_PAIR, l), lambda p, i: (i, p, 0)),
            pl.BlockSpec((1, l, HEAD_PAIR), lambda p, i: (i, 0, p)),
            pl.BlockSpec((1, HEAD_PAIR, l), lambda p, i: (i, p, 0)),
            pl.BlockSpec((1, c, HEAD_PAIR), lambda p, i: (i, 0, p)),
            pl.BlockSpec((1, HEAD_PAIR, c), lambda p, i: (i, p, 0)),
            pl.BlockSpec((2, n_combos, 2 * GRID_W), lambda p, i: (p, 0, 0)),
        ],
        out_specs=pl.BlockSpec((1, l, HEAD_PAIR), lambda p, i: (i, 0, p)),
        out_shape=jax.ShapeDtypeStruct((b, l, d), BF16),
        scratch_shapes=[
            pltpu.VMEM((2, 3, NA_KB, NA_QB), F32),
            pltpu.VMEM((PIPE_SLOTS, NA_KB, NA_QB), F32),
            pltpu.VMEM((PIPE_SLOTS, c, NA_QB), F32),
            pltpu.VMEM((PIPE_SLOTS, 1, NA_QB), F32),
        ],
        compiler_params=_params("arbitrary", "arbitrary"),
        name="neighbourhood_attention",
    )(qt, k, vt, kc, vct, bias_rows)


NA_MASKED_ROW = 2 * NA_KH - 1


@functools.lru_cache(maxsize=None)
def _na_bias_plan():
    kr = np.arange(NA_K_ROWS)[:, None]
    qr = np.arange(NA_Q_ROWS)[None, :]
    top = NA_KH - 1
    variants = [
        (kr < NA_KH + 0 * qr, kr - qr + top),
        ((kr >= qr) & (kr < qr + NA_KH), kr - qr + top - NA_KH // 2),
        (kr >= NA_K_ROWS - NA_KH + 0 * qr, kr - qr + top - (NA_K_ROWS - NA_Q_ROWS)),
    ]
    idx = np.stack([np.where(valid, off, NA_MASKED_ROW) for valid, off in variants])
    assert idx.min() >= 0 and idx.max() <= NA_MASKED_ROW
    combos = []
    plan = np.zeros((3, NA_K_ROWS, NA_Q_ROWS // 2), np.int32)
    for v in range(3):
        for r in range(NA_K_ROWS):
            for half in range(NA_Q_ROWS // 2):
                key = (int(idx[v, r, 2 * half]), int(idx[v, r, 2 * half + 1]))
                if key not in combos:
                    combos.append(key)
                plan[v, r, half] = combos.index(key)
    return plan, tuple(combos)


def _na_bias_rows(rpb):
    h, _, n_off = rpb.shape
    w = GRID_W
    reach = NA_KW - 1
    ext = jnp.concatenate([rpb, jnp.full((h, 1, n_off), NEG_INF, rpb.dtype)], axis=1)
    rev = ext[:, :, ::-1]
    gap = jnp.full((h, (2 * w - 2 * n_off) // 2), NEG_INF, rpb.dtype)
    _, combos = _na_bias_plan()
    rows = [jnp.concatenate([rev[:, a, reach:], gap, rev[:, b, :], gap, rev[:, a, :reach]], axis=-1)
            for a, b in combos]
    out = jnp.stack(rows, axis=1)
    assert out.shape[-1] == 2 * w
    return out


def _ctx_kernel(q_ref, k_ref, v_ref, o_ref, *, pairs):
    for p in range(pairs):
        cols = slice(p * HEAD_PAIR, (p + 1) * HEAD_PAIR)
        qb = q_ref[0, cols, :]
        kb = k_ref[0, :, cols]
        vb = v_ref[0, cols, :]
        outs = []
        for hh in range(2):
            s = jnp.dot(kb, _mask_rows_to_head(qb, hh), preferred_element_type=F32)
            pr = jnp.exp2(s - s.max(axis=0, keepdims=True))
            l = pr.sum(axis=0, keepdims=True)
            rows = slice(hh * HEAD_DIM, (hh + 1) * HEAD_DIM)
            outs.append(jnp.dot(vb[rows], pr.astype(BF16), preferred_element_type=F32) / l)
        o_ref[0, :, cols] = jnp.concatenate(outs, axis=0).T.astype(BF16)


def _ctx_attention(qct, kc, vct):
    b, d, c = qct.shape
    return pl.pallas_call(
        functools.partial(_ctx_kernel, pairs=d // HEAD_PAIR),
        grid=(b,),
        in_specs=[
            pl.BlockSpec((1, d, c), lambda i: (i, 0, 0)),
            pl.BlockSpec((1, c, d), lambda i: (i, 0, 0)),
            pl.BlockSpec((1, d, c), lambda i: (i, 0, 0)),
        ],
        out_specs=pl.BlockSpec((1, c, d), lambda i: (i, 0, 0)),
        out_shape=jax.ShapeDtypeStruct((b, c, d), BF16),
        compiler_params=_params("arbitrary"),
        name="context_attention",
    )(qct, kc, vct)


def _swa_kernel(q_ref, k_ref, v_ref, kc_ref, vc_ref, mask_ref, sink_ref, o_ref, sl_ref, sc_ref, m_ref, *,
                n_blocks, seq, unroll):
    kv_in_pair = pl.program_id(1) % 2
    ones_loc = jnp.ones((BF16_SUBLANES, SWA_KB), BF16)
    ones_ctx = jnp.ones((BF16_SUBLANES, kc_ref.shape[1]), BF16)

    def window(j):
        k0 = pl.multiple_of(jnp.clip(SWA_QB * j - SWA_WINDOW, 0, seq - SWA_KB), SWA_QB)
        return k0, pl.multiple_of(j * SWA_QB, SWA_QB)

    def scores(j, slot):
        j = jnp.minimum(j, n_blocks - 1)
        k0, q0 = window(j)
        variant = jnp.where(j == 0, 0, jnp.where(j == n_blocks - 1, 2, 1))
        qg = jnp.concatenate(
            [q_ref[0, g * HEAD_DIM:(g + 1) * HEAD_DIM, pl.ds(q0, SWA_QB)] for g in range(SWA_GROUP)],
            axis=1)
        qm = _mask_rows_to_head(jnp.concatenate([qg, qg], axis=0), kv_in_pair)
        mask = mask_ref[variant]
        s_loc = (jnp.dot(k_ref[0, pl.ds(k0, SWA_KB), :], qm, preferred_element_type=F32)
                 + jnp.concatenate([mask] * SWA_GROUP, axis=1))
        s_ctx = jnp.dot(kc_ref[0], qm, preferred_element_type=F32)
        sl_ref[slot] = s_loc
        sc_ref[slot] = s_ctx
        m_ref[slot] = jnp.maximum(_col_max(s_loc, s_ctx), sink_ref[0])

    def attend(j, slot):
        k0, q0 = window(j)
        m = m_ref[slot]
        p_loc = jnp.exp2(sl_ref[slot] - m).astype(BF16)
        p_ctx = jnp.exp2(sc_ref[slot] - m).astype(BF16)
        v_loc = jnp.concatenate([v_ref[0, :, pl.ds(k0, SWA_KB)], ones_loc], axis=0)
        v_ctx = jnp.concatenate([vc_ref[0], ones_ctx], axis=0)
        o = (jnp.dot(v_loc, p_loc, preferred_element_type=F32)
             + jnp.dot(v_ctx, p_ctx, preferred_element_type=F32))
        o = o[:HEAD_DIM] / (o[HEAD_DIM:HEAD_DIM + 1] + jnp.exp2(sink_ref[0] - m))
        o_rows = jnp.concatenate([o[:, g * SWA_QB:(g + 1) * SWA_QB] for g in range(SWA_GROUP)], axis=0)
        o_ref[0, pl.ds(q0, SWA_QB), :] = o_rows.T.astype(BF16)

    for n in range(PIPE_AHEAD):
        scores(n, n)

    def body(it, carry):
        base = it * unroll
        for n in range(unroll):
            scores(base + n + PIPE_AHEAD, (n + PIPE_AHEAD) % PIPE_SLOTS)
            attend(base + n, n % PIPE_SLOTS)
        return carry

    lax.fori_loop(0, n_blocks // unroll, body, 0)


def _swa_mask_tables():
    kr = np.arange(SWA_KB)[:, None]
    qc = np.arange(SWA_QB)[None, :]
    rel = [kr - qc, kr - SWA_WINDOW - qc, kr - (SWA_KB - SWA_QB) - qc]
    return np.stack([np.where(np.abs(r) <= SWA_WINDOW, 0.0, NEG_INF) for r in rel]).astype(np.float32)


def _swa_attention(qt, k, vt, kc, vct, sink_rows, unroll=8):
    b, d, l = qt.shape
    c = kc.shape[1]
    kv = k.shape[2] // HEAD_DIM
    gw = SWA_GROUP * HEAD_DIM
    n_blocks = l // SWA_QB
    assert l >= SWA_KB and n_blocks >= 3 and n_blocks % unroll == 0
    assert unroll % PIPE_SLOTS == 0 and PIPE_SLOTS > PIPE_AHEAD
    return pl.pallas_call(
        functools.partial(_swa_kernel, n_blocks=n_blocks, seq=l, unroll=unroll),
        grid=(b, kv),
        in_specs=[
            pl.BlockSpec((1, gw, l), lambda i, n: (i, n, 0)),
            pl.BlockSpec((1, l, HEAD_PAIR), lambda i, n: (i, 0, n // 2)),
            pl.BlockSpec((1, HEAD_DIM, l), lambda i, n: (i, n, 0)),
            pl.BlockSpec((1, c, HEAD_PAIR), lambda i, n: (i, 0, n // 2)),
            pl.BlockSpec((1, HEAD_DIM, c), lambda i, n: (i, n, 0)),
            _const_spec((3, SWA_KB, SWA_QB)),
            pl.BlockSpec((1, 1, SWA_GROUP * SWA_QB), lambda i, n: (n, 0, 0)),
        ],
        out_specs=pl.BlockSpec((1, l, gw), lambda i, n: (i, 0, n)),
        out_shape=jax.ShapeDtypeStruct((b, l, d), BF16),
        scratch_shapes=[
            pltpu.VMEM((PIPE_SLOTS, SWA_KB, SWA_GROUP * SWA_QB), F32),
            pltpu.VMEM((PIPE_SLOTS, c, SWA_GROUP * SWA_QB), F32),
            pltpu.VMEM((PIPE_SLOTS, 1, SWA_GROUP * SWA_QB), F32),
        ],
        compiler_params=_params("arbitrary", "arbitrary"),
        name="sliding_window_attention",
    )(qt, k, vt, kc, vct, jnp.asarray(_swa_mask_tables()), sink_rows)


def _post_kernel(a_ref, h_ref, mod_ref, g_ref, wo_ref, w1_ref, w2_ref, o_ref, *, ff_chunk):
    mod = mod_ref[0]
    y = jnp.dot(a_ref[0], wo_ref[...], preferred_element_type=F32)
    h1 = h_ref[0] + mod[2:3] * y
    u = _rms_modulate(h1, g_ref[...], mod[3:4], mod[4:5]).astype(BF16)
    d_ff = w1_ref.shape[1]
    acc = jnp.zeros_like(h1)
    for c0 in range(0, d_ff, ff_chunk):
        t = jnp.dot(u, w1_ref[:, c0:c0 + ff_chunk], preferred_element_type=F32)
        r = jnp.square(jnp.maximum(t, 0.0)).astype(BF16)
        acc = acc + jnp.dot(r, w2_ref[c0:c0 + ff_chunk, :], preferred_element_type=F32)
    o_ref[0] = h1 + mod[5:6] * acc


def _post_block(a, h, mod, g, wo, w1, w2, *, tm, ff_chunk=1024):
    b, t, d = h.shape
    return pl.pallas_call(
        functools.partial(_post_kernel, ff_chunk=ff_chunk),
        grid=(b, t // tm),
        in_specs=[
            pl.BlockSpec((1, tm, d), lambda i, j: (i, j, 0)),
            pl.BlockSpec((1, tm, d), lambda i, j: (i, j, 0)),
            pl.BlockSpec((1, 6, d), lambda i, j: (i, 0, 0)),
            _const_spec((1, d)),
            _const_spec(wo.shape),
            _const_spec(w1.shape),
            _const_spec(w2.shape),
        ],
        out_specs=pl.BlockSpec((1, tm, d), lambda i, j: (i, j, 0)),
        out_shape=jax.ShapeDtypeStruct((b, t, d), F32),
        compiler_params=_params("arbitrary", "arbitrary"),
        name="outproj_mlp",
    )(a, h, mod, g, wo, w1, w2)


def _rope_tables_t(l):
    t = jnp.arange(l, dtype=jnp.int32)
    row = (t // GRID_W).astype(F32)
    col = (t % GRID_W).astype(F32)
    n_freq = HEAD_DIM // 4
    inv = ROPE_BASE ** (-jnp.arange(n_freq, dtype=F32) / n_freq)
    ang = jnp.stack([row[:, None] * inv, col[:, None] * inv], axis=1)
    cos = jnp.transpose(jnp.cos(ang), (1, 2, 0))
    sin = jnp.transpose(jnp.sin(ang), (1, 2, 0))
    cos_t = jnp.stack([cos, cos], axis=1).reshape(HEAD_DIM, l)
    sin_t = jnp.stack([-sin, sin], axis=1).reshape(HEAD_DIM, l)
    return cos_t, sin_t


def _gain_cols(gain, tm, scale=1.0):
    return jnp.broadcast_to((gain.astype(F32) * scale)[:, None], (HEAD_DIM, tm))


def kernel(x, c, ctx, c_ctx, ada_w, ada_b, g_mix, g_mlp, mlp_w1, mlp_w2, na_wqkv, na_q_gain, na_k_gain,
           na_rpb, na_wo, swa_wqkv, swa_q_gain, swa_k_gain, swa_sink, swa_wo):
    b, l, d = x.shape
    n_ctx = ctx.shape[1]
    tm = 512
    tmc = n_ctx
    scale = HEAD_DIM ** -0.5 * LOG2E

    cond =jnp.concatenate([c, c_ctx[None], jnp.zeros((8 - b - 1, d), F32)], axis=0)
    mods = _modulation(cond, ada_w, ada_b)

    def mods_of(i):
        lat = mods[i, :b].reshape(b, 6, d)
        cx = jnp.broadcast_to(mods[i, b].reshape(1, 6, d), (b, 6, d))
        return lat, cx

    def row(v):
        return v.reshape(1, d)

    mod_lat, mod_ctx = mods_of(0)
    wt = na_wqkv[0].T.astype(BF16)
    gq, gk = na_q_gain[0], na_k_gain[0]
    qt, k, vt = _qkv_project(x, mod_lat, row(g_mix[0]), wt, _gain_cols(gq, tm, scale), _gain_cols(gk, tm),
                             None, kv_width=d, tm=tm)
    qct, kc, vct = _qkv_project(ctx, mod_ctx, row(g_mix[0]), wt, _gain_cols(gq, tmc, scale),
                                _gain_cols(gk, tmc), None, kv_width=d, tm=tmc)
    a_lat = _na_attention(qt, k, vt, kc, vct, _na_bias_rows(na_rpb[0] * LOG2E))
    a_ctx = _ctx_attention(qct, kc, vct)
    wo = na_wo[0].astype(BF16)
    w1 = mlp_w1[0].astype(BF16)
    w2 = mlp_w2[0].astype(BF16)
    h_lat = _post_block(a_lat, x, mod_lat, row(g_mlp[0]), wo, w1, w2, tm=tm)
    h_ctx = _post_block(a_ctx, ctx, mod_ctx, row(g_mlp[0]), wo, w1, w2, tm=tmc)

    mod_lat, mod_ctx = mods_of(1)
    kvw = SWA_KV_HEADS * HEAD_DIM
    wt = swa_wqkv[0].T.astype(BF16)
    gq, gk = swa_q_gain[0], swa_k_gain[0]
    cos_t, sin_t = _rope_tables_t(l)
    qt, k, vt = _qkv_project(h_lat, mod_lat, row(g_mix[1]), wt, _gain_cols(gq, tm, scale), _gain_cols(gk, tm),
                             (cos_t, sin_t), kv_width=kvw, tm=tm)
    _, kc, vct = _qkv_project(h_ctx, mod_ctx, row(g_mix[1]), wt, _gain_cols(gq, tmc, scale),
                              _gain_cols(gk, tmc), None, kv_width=kvw, tm=tmc)
    sink_rows = jnp.repeat(swa_sink[0].astype(F32) * LOG2E, SWA_QB).reshape(SWA_KV_HEADS, 1, SWA_GROUP * SWA_QB)
    a_lat = _swa_attention(qt, k, vt, kc, vct, sink_rows)
    return _post_block(a_lat, h_lat, mod_lat, row(g_mlp[1]), swa_wo[0].astype(BF16),
                       mlp_w1[1].astype(BF16), mlp_w2[1].astype(BF16), tm=tm)
```

```python
import functools

import numpy as np
import jax
import jax.numpy as jnp
from jax import lax
from jax.experimental import pallas as pl
from jax.experimental.pallas import tpu as pltpu

GRID_W = 64
N_HEADS = 16
HEAD_DIM = 64
NA_KH = 8
NA_KW = 16
SWA_KV_HEADS = 4
SWA_GROUP = N_HEADS // SWA_KV_HEADS
SWA_WINDOW = 128
ROPE_BASE = 10000.0
NORM_EPS = 1e-6
NEG_INF = -1e30
LOG2E = float(np.log2(np.e))

F32 = jnp.float32
BF16 = jnp.bfloat16

V7X_VMEM_BYTES = 64 * 1024 * 1024
VMEM_LIMIT_BYTES = V7X_VMEM_BYTES - 8 * 1024 * 1024

BF16_SUBLANES = 16
HEAD_PAIR = 2 * HEAD_DIM
NA_Q_ROWS = 4
NA_K_ROWS = NA_Q_ROWS + NA_KH
NA_QB = NA_Q_ROWS * GRID_W
NA_KB = NA_K_ROWS * GRID_W
SWA_QB = 128
SWA_KB = SWA_QB + 2 * SWA_WINDOW
TM_QKV = 1024
TM_POST = 512
QKV_SUB = 256
PIPE_AHEAD = 2
PIPE_SLOTS = 4


def _params(*semantics):
    return pltpu.CompilerParams(dimension_semantics=semantics, vmem_limit_bytes=VMEM_LIMIT_BYTES)


def _const_spec(shape):
    zeros = (0,) * len(shape)
    return pl.BlockSpec(shape, lambda *_: zeros, pipeline_mode=pl.Buffered(1))


def _mod_kernel(cond_ref, w_ref, b_ref, o_ref):
    s = jax.nn.silu(cond_ref[...])
    o_ref[0] = jnp.dot(s, w_ref[0], precision=lax.Precision.HIGHEST,
                       preferred_element_type=F32) + b_ref[0]


def _modulation(cond, ada_w, ada_b, tn=1536):
    depth, d, n = ada_w.shape
    rows = cond.shape[0]
    return pl.pallas_call(
        _mod_kernel,
        grid=(depth, n // tn),
        in_specs=[
            pl.BlockSpec((rows, d), lambda i, j: (0, 0)),
            pl.BlockSpec((1, d, tn), lambda i, j: (i, 0, j)),
            pl.BlockSpec((1, 1, tn), lambda i, j: (i, 0, j)),
        ],
        out_specs=pl.BlockSpec((1, rows, tn), lambda i, j: (i, 0, j)),
        out_shape=jax.ShapeDtypeStruct((depth, rows, n), F32),
        compiler_params=_params("arbitrary", "arbitrary"),
        name="adaln_modulation",
    )(cond, ada_w, ada_b.reshape(depth, 1, n))


def _rms_modulate(x, g, shift, scale):
    y = x * lax.rsqrt(jnp.mean(x * x, axis=-1, keepdims=True) + NORM_EPS) * g
    return y * (1.0 + scale) + shift


def _head_norm_t(a, gain_b):
    n = a.shape[0] // HEAD_DIM
    tm = a.shape[1]
    a = a.reshape(n, HEAD_DIM, tm)
    a = a * lax.rsqrt(jnp.mean(a * a, axis=1, keepdims=True) + NORM_EPS) * gain_b[None]
    return a.reshape(n * HEAD_DIM, tm)


def _head_norm_rope_t(a, gain_cos, gain_sin):
    n = a.shape[0] // HEAD_DIM
    tm = a.shape[1]
    a = a.reshape(n, HEAD_DIM, tm)
    a = (a * lax.rsqrt(jnp.mean(a * a, axis=1, keepdims=True) + NORM_EPS)).astype(BF16)
    q = HEAD_DIM // 4
    swapped = jnp.concatenate([a[:, q:2 * q], a[:, 0:q], a[:, 3 * q:4 * q], a[:, 2 * q:3 * q]], axis=1)
    a = a * gain_cos[None] + swapped * gain_sin[None]
    return a.reshape(n * HEAD_DIM, tm)


def _qkv_kernel(*refs, d_model, kv_width, use_rope, chunk, sub):
    if use_rope:
        h_ref, mod_ref, g_ref, wt_ref, qc_ref, qs_ref, kc_ref, ks_ref, q_ref, k_ref, v_ref = refs
    else:
        h_ref, mod_ref, g_ref, wt_ref, gq_ref, gk_ref, q_ref, k_ref, v_ref = refs
        gq = gq_ref[...]
        gk = gk_ref[...]
    mod = mod_ref[0]
    g_eff = (g_ref[...] * (1.0 + mod[1:2])).astype(BF16)
    shift = mod[0:1].astype(BF16)
    contract_last = (((1,), (1,)), ((), ()))
    kv_chunk = min(chunk, kv_width)
    for t0 in range(0, h_ref.shape[1], sub):
        toks = slice(t0, t0 + sub)
        x = h_ref[0, toks, :]
        xn = (x * lax.rsqrt(jnp.mean(x * x, axis=-1, keepdims=True) + NORM_EPS)).astype(BF16)
        u = xn * g_eff + shift

        def proj(r0, rows):
            return lax.dot_general(wt_ref[r0:r0 + rows, :], u, contract_last,
                                   preferred_element_type=F32)

        def norm_q(a):
            if use_rope:
                return _head_norm_rope_t(a, qc_ref[:, toks], qs_ref[:, toks])
            return _head_norm_t(a, gq).astype(BF16)

        def norm_k(a):
            if use_rope:
                return _head_norm_rope_t(a, kc_ref[:, toks], ks_ref[:, toks])
            return _head_norm_t(a, gk).astype(BF16)

        for r0 in range(0, d_model, chunk):
            q_ref[0, r0:r0 + chunk, toks] = norm_q(proj(r0, chunk))
        for r0 in range(0, kv_width, kv_chunk):
            k_ref[0, toks, r0:r0 + kv_chunk] = norm_k(proj(d_model + r0, kv_chunk)).T
        for r0 in range(0, kv_width, kv_chunk):
            v_ref[0, r0:r0 + kv_chunk, toks] = proj(d_model + kv_width + r0, kv_chunk).astype(BF16)


def _qkv_project(h, mod, g, wt, gains, rope_tabs, *, kv_width, tm, chunk=256, sub=QKV_SUB):
    b, t, d = h.shape
    use_rope = rope_tabs is not None
    in_specs = [
        pl.BlockSpec((1, tm, d), lambda i, j: (i, j, 0)),
        pl.BlockSpec((1, 6, d), lambda i, j: (i, 0, 0)),
        _const_spec((1, d)),
        _const_spec(wt.shape),
    ]
    args = [h, mod, g, wt]
    if use_rope:
        in_specs += [pl.BlockSpec((HEAD_DIM, tm), lambda i, j: (0, j))] * 4
        args += list(rope_tabs)
    else:
        in_specs += [_const_spec((HEAD_DIM, sub))] * 2
        args += list(gains)
    return pl.pallas_call(
        functools.partial(_qkv_kernel, d_model=d, kv_width=kv_width, use_rope=use_rope, chunk=chunk,
                          sub=sub),
        grid=(b, t // tm),
        in_specs=in_specs,
        out_specs=[
            pl.BlockSpec((1, d, tm), lambda i, j: (i, 0, j)),
            pl.BlockSpec((1, tm, kv_width), lambda i, j: (i, j, 0)),
            pl.BlockSpec((1, kv_width, tm), lambda i, j: (i, 0, j)),
        ],
        out_shape=[
            jax.ShapeDtypeStruct((b, d, t), BF16),
            jax.ShapeDtypeStruct((b, t, kv_width), BF16),
            jax.ShapeDtypeStruct((b, kv_width, t), BF16),
        ],
        compiler_params=_params("arbitrary", "arbitrary"),
        name="qkv_rope" if use_rope else "qkv",
    )(*args)


def _col_max(*parts):
    m = parts[0].max(axis=0, keepdims=True)
    for p in parts[1:]:
        m = jnp.maximum(m, p.max(axis=0, keepdims=True))
    return m


def _mask_rows_to_head(q_pair, head):
    row = lax.broadcasted_iota(jnp.int32, q_pair.shape, 0)
    return jnp.where((row // HEAD_DIM) == head, q_pair, jnp.zeros_like(q_pair))


def _na_build_bias(rows_ref, bias_ref):
    w = GRID_W
    kc = lax.broadcasted_iota(jnp.int32, (w, 2 * w), 0)
    qc = lax.broadcasted_iota(jnp.int32, (w, 2 * w), 1) % w
    c0 = jnp.clip(qc - NA_KW // 2, 0, w - NA_KW)
    col_valid = (kc >= c0) & (kc < c0 + NA_KW)
    plan, combos = _na_bias_plan()
    for hh in range(2):
        tiles = []
        for n in range(len(combos)):
            row = jnp.broadcast_to(rows_ref[hh, n:n + 1, :], (w, 2 * w))
            tiles.append(jnp.where(col_valid, pltpu.roll(row, 0, 1, stride=1, stride_axis=0), NEG_INF))
        for v in range(3):
            for kr in range(NA_K_ROWS):
                for half in range(NA_Q_ROWS // 2):
                    bias_ref[hh, v, kr * w:(kr + 1) * w, half * 2 * w:(half + 1) * 2 * w] = tiles[plan[v, kr, half]]


def _na_kernel(q_ref, k_ref, v_ref, kc_ref, vc_ref, rows_ref, o_ref, bias_ref, sl_ref, sc_ref, m_ref, *,
               n_blocks, grid_rows, unroll):
    @pl.when(pl.program_id(1) == 0)
    def _():
        _na_build_bias(rows_ref, bias_ref)

    ones_loc = jnp.ones((BF16_SUBLANES, NA_KB), BF16)
    ones_ctx = jnp.ones((BF16_SUBLANES, kc_ref.shape[1]), BF16)

    def window(i):
        start_row = jnp.clip(NA_Q_ROWS * i - NA_KH // 2, 0, grid_rows - NA_K_ROWS)
        return pl.multiple_of(start_row * GRID_W, NA_QB), pl.multiple_of(i * NA_QB, NA_QB)

    def scores(i, hh, slot):
        i = jnp.minimum(i, n_blocks - 1)
        k0, q0 = window(i)
        variant = jnp.where(i == 0, 0, jnp.where(i == n_blocks - 1, 2, 1))
        qm = _mask_rows_to_head(q_ref[0, :, pl.ds(q0, NA_QB)], hh)
        s_loc = jnp.dot(k_ref[0, pl.ds(k0, NA_KB), :], qm, preferred_element_type=F32) + bias_ref[hh, variant]
        s_ctx = jnp.dot(kc_ref[0], qm, preferred_element_type=F32)
        sl_ref[slot] = s_loc
        sc_ref[slot] = s_ctx
        m_ref[slot] = _col_max(s_loc, s_ctx)

    def attend(i, hh, slot):
        k0, _ = window(i)
        m = m_ref[slot]
        p_loc = jnp.exp2(sl_ref[slot] - m).astype(BF16)
        p_ctx = jnp.exp2(sc_ref[slot] - m).astype(BF16)
        rows = slice(hh * HEAD_DIM, (hh + 1) * HEAD_DIM)
        v_loc = jnp.concatenate([v_ref[0, rows, pl.ds(k0, NA_KB)], ones_loc], axis=0)
        v_ctx = jnp.concatenate([vc_ref[0, rows, :], ones_ctx], axis=0)
        o = (jnp.dot(v_loc, p_loc, preferred_element_type=F32)
             + jnp.dot(v_ctx, p_ctx, preferred_element_type=F32))
        return o[:HEAD_DIM] / o[HEAD_DIM:HEAD_DIM + 1]

    for n in range(PIPE_AHEAD):
        scores(n // 2, n % 2, n)

    def body(it, carry):
        base = it * unroll
        units = [(base + blk, hh) for blk in range(unroll + (PIPE_AHEAD + 1) // 2) for hh in range(2)]
        pair = []
        for n, (i, hh) in enumerate(units[:2 * unroll]):
            ahead = units[n + PIPE_AHEAD]
            scores(ahead[0], ahead[1], (n + PIPE_AHEAD) % PIPE_SLOTS)
            pair.append(attend(i, hh, n % PIPE_SLOTS))
            if hh == 1:
                q0 = pl.multiple_of(i * NA_QB, NA_QB)
                o_ref[0, pl.ds(q0, NA_QB), :] = jnp.concatenate(pair, axis=0).T.astype(BF16)
                pair = []
        return carry

    lax.fori_loop(0, n_blocks // unroll, body, 0)


def _na_attention(qt, k, vt, kc, vct, bias_rows, unroll=16):
    b, d, l = qt.shape
    c = kc.shape[1]
    pairs = d // HEAD_PAIR
    grid_rows = l // GRID_W
    n_blocks = grid_rows // NA_Q_ROWS
    n_combos = bias_rows.shape[1]
    assert grid_rows >= NA_K_ROWS and n_blocks >= 3 and n_blocks % unroll == 0
    assert (2 * unroll) % PIPE_SLOTS == 0 and PIPE_SLOTS > PIPE_AHEAD
    return pl.pallas_call(
        functools.partial(_na_kernel, n_blocks=n_blocks, grid_rows=grid_rows, unroll=unroll),
        grid=(pairs, b),
        in_specs=[
            pl.BlockSpec((1, HEAD_PAIR, l), lambda p, i: (i, p, 0)),
            pl.BlockSpec((1, l, HEAD_PAIR), lambda p, i: (i, 0, p)),
            pl.BlockSpec((1, HEAD_PAIR, l), lambda p, i: (i, p, 0)),
            pl.BlockSpec((1, c, HEAD_PAIR), lambda p, i: (i, 0, p)),
            pl.BlockSpec((1, HEAD_PAIR, c), lambda p, i: (i, p, 0)),
            pl.BlockSpec((2, n_combos, 2 * GRID_W), lambda p, i: (p, 0, 0)),
        ],
        out_specs=pl.BlockSpec((1, l, HEAD_PAIR), lambda p, i: (i, 0, p)),
        out_shape=jax.ShapeDtypeStruct((b, l, d), BF16),
        scratch_shapes=[
            pltpu.VMEM((2, 3, NA_KB, NA_QB), F32),
            pltpu.VMEM((PIPE_SLOTS, NA_KB, NA_QB), F32),
            pltpu.VMEM((PIPE_SLOTS, c, NA_QB), F32),
            pltpu.VMEM((PIPE_SLOTS, 1, NA_QB), F32),
        ],
        compiler_params=_params("arbitrary", "arbitrary"),
        name="neighbourhood_attention",
    )(qt, k, vt, kc, vct, bias_rows)


NA_MASKED_ROW = 2 * NA_KH - 1


@functools.lru_cache(maxsize=None)
def _na_bias_plan():
    kr = np.arange(NA_K_ROWS)[:, None]
    qr = np.arange(NA_Q_ROWS)[None, :]
    top = NA_KH - 1
    variants = [
        (kr < NA_KH + 0 * qr, kr - qr + top),
        ((kr >= qr) & (kr < qr + NA_KH), kr - qr + top - NA_KH // 2),
        (kr >= NA_K_ROWS - NA_KH + 0 * qr, kr - qr + top - (NA_K_ROWS - NA_Q_ROWS)),
    ]
    idx = np.stack([np.where(valid, off, NA_MASKED_ROW) for valid, off in variants])
    assert idx.min() >= 0 and idx.max() <= NA_MASKED_ROW
    combos = []
    plan = np.zeros((3, NA_K_ROWS, NA_Q_ROWS // 2), np.int32)
    for v in range(3):
        for r in range(NA_K_ROWS):
            for half in range(NA_Q_ROWS // 2):
                key = (int(idx[v, r, 2 * half]), int(idx[v, r, 2 * half + 1]))
                if key not in combos:
                    combos.append(key)
                plan[v, r, half] = combos.index(key)
    return plan, tuple(combos)


def _na_bias_rows(rpb):
    h, _, n_off = rpb.shape
    w = GRID_W
    reach = NA_KW - 1
    ext = jnp.concatenate([rpb, jnp.full((h, 1, n_off), NEG_INF, rpb.dtype)], axis=1)
    rev = ext[:, :, ::-1]
    gap = jnp.full((h, (2 * w - 2 * n_off) // 2), NEG_INF, rpb.dtype)
    _, combos = _na_bias_plan()
    rows = [jnp.concatenate([rev[:, a, reach:], gap, rev[:, b, :], gap, rev[:, a, :reach]], axis=-1)
            for a, b in combos]
    out = jnp.stack(rows, axis=1)
    assert out.shape[-1] == 2 * w
    return out


def _ctx_kernel(q_ref, k_ref, v_ref, o_ref, *, pairs):
    for p in range(pairs):
        cols = slice(p * HEAD_PAIR, (p + 1) * HEAD_PAIR)
        qb = q_ref[0, cols, :]
        kb = k_ref[0, :, cols]
        vb = v_ref[0, cols, :]
        outs = []
        for hh in range(2):
            s = jnp.dot(kb, _mask_rows_to_head(qb, hh), preferred_element_type=F32)
            pr = jnp.exp2(s - s.max(axis=0, keepdims=True))
            l = pr.sum(axis=0, keepdims=True)
            rows = slice(hh * HEAD_DIM, (hh + 1) * HEAD_DIM)
            outs.append(jnp.dot(vb[rows], pr.astype(BF16), preferred_element_type=F32) / l)
        o_ref[0, :, cols] = jnp.concatenate(outs, axis=0).T.astype(BF16)


def _ctx_attention(qct, kc, vct):
    b, d, c = qct.shape
    return pl.pallas_call(
        functools.partial(_ctx_kernel, pairs=d // HEAD_PAIR),
        grid=(b,),
        in_specs=[
            pl.BlockSpec((1, d, c), lambda i: (i, 0, 0)),
            pl.BlockSpec((1, c, d), lambda i: (i, 0, 0)),
            pl.BlockSpec((1, d, c), lambda i: (i, 0, 0)),
        ],
        out_specs=pl.BlockSpec((1, c, d), lambda i: (i, 0, 0)),
        out_shape=jax.ShapeDtypeStruct((b, c, d), BF16),
        compiler_params=_params("arbitrary"),
        name="context_attention",
    )(qct, kc, vct)


def _swa_kernel(q_ref, k_ref, v_ref, kc_ref, vc_ref, mask_ref, sink_ref, o_ref, sl_ref, sc_ref, m_ref, *,
                n_blocks, seq, unroll):
    kv_in_pair = pl.program_id(1) % 2
    ones_loc = jnp.ones((BF16_SUBLANES, SWA_KB), BF16)
    ones_ctx = jnp.ones((BF16_SUBLANES, kc_ref.shape[1]), BF16)

    def window(j):
        k0 = pl.multiple_of(jnp.clip(SWA_QB * j - SWA_WINDOW, 0, seq - SWA_KB), SWA_QB)
        return k0, pl.multiple_of(j * SWA_QB, SWA_QB)

    def scores(j, slot):
        j = jnp.minimum(j, n_blocks - 1)
        k0, q0 = window(j)
        variant = jnp.where(j == 0, 0, jnp.where(j == n_blocks - 1, 2, 1))
        qg = jnp.concatenate(
            [q_ref[0, g * HEAD_DIM:(g + 1) * HEAD_DIM, pl.ds(q0, SWA_QB)] for g in range(SWA_GROUP)],
            axis=1)
        qm = _mask_rows_to_head(jnp.concatenate([qg, qg], axis=0), kv_in_pair)
        mask = mask_ref[variant]
        s_loc = (jnp.dot(k_ref[0, pl.ds(k0, SWA_KB), :], qm, preferred_element_type=F32)
                 + jnp.concatenate([mask] * SWA_GROUP, axis=1))
        s_ctx = jnp.dot(kc_ref[0], qm, preferred_element_type=F32)
        sl_ref[slot] = s_loc
        sc_ref[slot] = s_ctx
        m_ref[slot] = jnp.maximum(_col_max(s_loc, s_ctx), sink_ref[0])

    def attend(j, slot):
        k0, q0 = window(j)
        m = m_ref[slot]
        p_loc = jnp.exp2(sl_ref[slot] - m).astype(BF16)
        p_ctx = jnp.exp2(sc_ref[slot] - m).astype(BF16)
        v_loc = jnp.concatenate([v_ref[0, :, pl.ds(k0, SWA_KB)], ones_loc], axis=0)
        v_ctx = jnp.concatenate([vc_ref[0], ones_ctx], axis=0)
        o = (jnp.dot(v_loc, p_loc, preferred_element_type=F32)
             + jnp.dot(v_ctx, p_ctx, preferred_element_type=F32))
        o = o[:HEAD_DIM] / (o[HEAD_DIM:HEAD_DIM + 1] + jnp.exp2(sink_ref[0] - m))
        o_rows = jnp.concatenate([o[:, g * SWA_QB:(g + 1) * SWA_QB] for g in range(SWA_GROUP)], axis=0)
        o_ref[0, pl.ds(q0, SWA_QB), :] = o_rows.T.astype(BF16)

    for n in range(PIPE_AHEAD):
        scores(n, n)

    def body(it, carry):
        base = it * unroll
        for n in range(unroll):
            scores(base + n + PIPE_AHEAD, (n + PIPE_AHEAD) % PIPE_SLOTS)
            attend(base + n, n % PIPE_SLOTS)
        return carry

    lax.fori_loop(0, n_blocks // unroll, body, 0)


def _swa_mask_tables():
    kr = np.arange(SWA_KB)[:, None]
    qc = np.arange(SWA_QB)[None, :]
    rel = [kr - qc, kr - SWA_WINDOW - qc, kr - (SWA_KB - SWA_QB) - qc]
    return np.stack([np.where(np.abs(r) <= SWA_WINDOW, 0.0, NEG_INF) for r in rel]).astype(np.float32)


def _swa_attention(qt, k, vt, kc, vct, sink_rows, unroll=16):
    b, d, l = qt.shape
    c = kc.shape[1]
    kv = k.shape[2] // HEAD_DIM
    gw = SWA_GROUP * HEAD_DIM
    n_blocks = l // SWA_QB
    assert l >= SWA_KB and n_blocks >= 3 and n_blocks % unroll == 0
    assert unroll % PIPE_SLOTS == 0 and PIPE_SLOTS > PIPE_AHEAD
    return pl.pallas_call(
        functools.partial(_swa_kernel, n_blocks=n_blocks, seq=l, unroll=unroll),
        grid=(b, kv),
        in_specs=[
            pl.BlockSpec((1, gw, l), lambda i, n: (i, n, 0)),
            pl.BlockSpec((1, l, HEAD_PAIR), lambda i, n: (i, 0, n // 2)),
            pl.BlockSpec((1, HEAD_DIM, l), lambda i, n: (i, n, 0)),
            pl.BlockSpec((1, c, HEAD_PAIR), lambda i, n: (i, 0, n // 2)),
            pl.BlockSpec((1, HEAD_DIM, c), lambda i, n: (i, n, 0)),
            _const_spec((3, SWA_KB, SWA_QB)),
            pl.BlockSpec((1, 1, SWA_GROUP * SWA_QB), lambda i, n: (n, 0, 0)),
        ],
        out_specs=pl.BlockSpec((1, l, gw), lambda i, n: (i, 0, n)),
        out_shape=jax.ShapeDtypeStruct((b, l, d), BF16),
        scratch_shapes=[
            pltpu.VMEM((PIPE_SLOTS, SWA_KB, SWA_GROUP * SWA_QB), F32),
            pltpu.VMEM((PIPE_SLOTS, c, SWA_GROUP * SWA_QB), F32),
            pltpu.VMEM((PIPE_SLOTS, 1, SWA_GROUP * SWA_QB), F32),
        ],
        compiler_params=_params("arbitrary", "arbitrary"),
        name="sliding_window_attention",
    )(qt, k, vt, kc, vct, jnp.asarray(_swa_mask_tables()), sink_rows)


def _post_kernel(a_ref, h_ref, mod_ref, g_ref, wo_ref, w1_ref, w2_ref, o_ref, *, ff_chunk):
    mod = mod_ref[0]
    y = jnp.dot(a_ref[0], wo_ref[...], preferred_element_type=F32)
    h1 = h_ref[0] + mod[2:3] * y
    u = _rms_modulate(h1, g_ref[...], mod[3:4], mod[4:5]).astype(BF16)
    d_ff = w1_ref.shape[1]
    acc = jnp.zeros_like(h1)
    for c0 in range(0, d_ff, ff_chunk):
        t = jnp.dot(u, w1_ref[:, c0:c0 + ff_chunk], preferred_element_type=F32)
        r = jnp.square(jnp.maximum(t, 0.0)).astype(BF16)
        acc = acc + jnp.dot(r, w2_ref[c0:c0 + ff_chunk, :], preferred_element_type=F32)
    o_ref[0] = h1 + mod[5:6] * acc


def _post_block(a, h, mod, g, wo, w1, w2, *, tm, ff_chunk=1024):
    b, t, d = h.shape
    return pl.pallas_call(
        functools.partial(_post_kernel, ff_chunk=ff_chunk),
        grid=(b, t // tm),
        in_specs=[
            pl.BlockSpec((1, tm, d), lambda i, j: (i, j, 0)),
            pl.BlockSpec((1, tm, d), lambda i, j: (i, j, 0)),
            pl.BlockSpec((1, 6, d), lambda i, j: (i, 0, 0)),
            _const_spec((1, d)),
            _const_spec(wo.shape),
            _const_spec(w1.shape),
            _const_spec(w2.shape),
        ],
        out_specs=pl.BlockSpec((1, tm, d), lambda i, j: (i, j, 0)),
        out_shape=jax.ShapeDtypeStruct((b, t, d), F32),
        compiler_params=_params("arbitrary", "arbitrary"),
        name="outproj_mlp",
    )(a, h, mod, g, wo, w1, w2)


def _rope_tables_t(l):
    t = jnp.arange(l, dtype=jnp.int32)
    row = (t // GRID_W).astype(F32)
    col = (t % GRID_W).astype(F32)
    n_freq = HEAD_DIM // 4
    inv = ROPE_BASE ** (-jnp.arange(n_freq, dtype=F32) / n_freq)
    ang = jnp.stack([row[:, None] * inv, col[:, None] * inv], axis=1)
    cos = jnp.transpose(jnp.cos(ang), (1, 2, 0))
    sin = jnp.transpose(jnp.sin(ang), (1, 2, 0))
    cos_t = jnp.stack([cos, cos], axis=1).reshape(HEAD_DIM, l)
    sin_t = jnp.stack([-sin, sin], axis=1).reshape(HEAD_DIM, l)
    return cos_t, sin_t


def _swap_rotary_halves(v):
    q = HEAD_DIM // 4
    return jnp.concatenate([v[q:2 * q], v[0:q], v[3 * q:4 * q], v[2 * q:3 * q]], axis=0)


def _gain_rope_tables(gain, cos_t, sin_t):
    gain = gain.astype(F32)
    return ((gain[:, None] * cos_t).astype(BF16), (_swap_rotary_halves(gain)[:, None] * sin_t).astype(BF16))


def _gain_cols(gain, scale=1.0):
    return jnp.broadcast_to((gain.astype(F32) * scale)[:, None], (HEAD_DIM, QKV_SUB))


def kernel(x, c, ctx, c_ctx, ada_w, ada_b, g_mix, g_mlp, mlp_w1, mlp_w2, na_wqkv, na_q_gain, na_k_gain,
           na_rpb, na_wo, swa_wqkv, swa_q_gain, swa_k_gain, swa_sink, swa_wo):
    b, l, d = x.shape
    n_ctx = ctx.shape[1]
    tm = TM_POST
    tmc = n_ctx
    scale = HEAD_DIM ** -0.5 * LOG2E

    cond =jnp.concatenate([c, c_ctx[None], jnp.zeros((8 - b - 1, d), F32)], axis=0)
    mods = _modulation(cond, ada_w, ada_b)

    def mods_of(i):
        lat = mods[i, :b].reshape(b, 6, d)
        cx = jnp.broadcast_to(mods[i, b].reshape(1, 6, d), (b, 6, d))
        return lat, cx

    def row(v):
        return v.reshape(1, d)

    mod_lat, mod_ctx = mods_of(0)
    wt = na_wqkv[0].T.astype(BF16)
    gq, gk = na_q_gain[0], na_k_gain[0]
    gains = (_gain_cols(gq, scale), _gain_cols(gk))
    qt, k, vt = _qkv_project(x, mod_lat, row(g_mix[0]), wt, gains, None, kv_width=d, tm=TM_QKV)
    qct, kc, vct = _qkv_project(ctx, mod_ctx, row(g_mix[0]), wt, gains, None, kv_width=d, tm=tmc)
    a_lat = _na_attention(qt, k, vt, kc, vct, _na_bias_rows(na_rpb[0] * LOG2E))
    a_ctx = _ctx_attention(qct, kc, vct)
    wo = na_wo[0].astype(BF16)
    w1 = mlp_w1[0].astype(BF16)
    w2 = mlp_w2[0].astype(BF16)
    h_lat = _post_block(a_lat, x, mod_lat, row(g_mlp[0]), wo, w1, w2, tm=tm)
    h_ctx = _post_block(a_ctx, ctx, mod_ctx, row(g_mlp[0]), wo, w1, w2, tm=tmc)

    mod_lat, mod_ctx = mods_of(1)
    kvw = SWA_KV_HEADS * HEAD_DIM
    wt = swa_wqkv[0].T.astype(BF16)
    gq, gk = swa_q_gain[0], swa_k_gain[0]
    cos_t, sin_t = _rope_tables_t(l)
    rope_tabs = (_gain_rope_tables(gq * scale, cos_t, sin_t) + _gain_rope_tables(gk, cos_t, sin_t))
    qt, k, vt = _qkv_project(h_lat, mod_lat, row(g_mix[1]), wt, None, rope_tabs, kv_width=kvw, tm=TM_QKV)
    gains = (_gain_cols(gq, scale), _gain_cols(gk))
    _, kc, vct = _qkv_project(h_ctx, mod_ctx, row(g_mix[1]), wt, gains, None, kv_width=kvw, tm=tmc)
    sink_rows = jnp.repeat(swa_sink[0].astype(F32) * LOG2E, SWA_QB).reshape(SWA_KV_HEADS, 1, SWA_GROUP * SWA_QB)
    a_lat = _swa_attention(qt, k, vt, kc, vct, sink_rows)
    return _post_block(a_lat, h_lat, mod_lat, row(g_mlp[1]), swa_wo[0].astype(BF16),
                       mlp_w1[1].astype(BF16), mlp_w2[1].astype(BF16), tm=tm)
```

```python
import functools

import numpy as np
import jax
import jax.numpy as jnp
from jax import lax
from jax.experimental import pallas as pl
from jax.experimental.pallas import tpu as pltpu

GRID_W = 64
N_HEADS = 16
HEAD_DIM = 64
NA_KH = 8
NA_KW = 16
SWA_KV_HEADS = 4
SWA_GROUP = N_HEADS // SWA_KV_HEADS
SWA_WINDOW = 128
ROPE_BASE = 10000.0
NORM_EPS = 1e-6
NEG_INF = -1e30
LOG2E = float(np.log2(np.e))

F32 = jnp.float32
BF16 = jnp.bfloat16

V7X_VMEM_BYTES = 64 * 1024 * 1024
VMEM_LIMIT_BYTES = V7X_VMEM_BYTES - 8 * 1024 * 1024

BF16_SUBLANES = 16
HEAD_PAIR = 2 * HEAD_DIM
NA_Q_ROWS = 4
NA_K_ROWS = NA_Q_ROWS + NA_KH
NA_QB = NA_Q_ROWS * GRID_W
NA_KB = NA_K_ROWS * GRID_W
SWA_QB = 128
SWA_KB = SWA_QB + 2 * SWA_WINDOW
TM_QKV = 1024
TM_POST = 512
QKV_SUB = 256
QKV_CHUNK = 256
QKV_CHUNK_ROPE = 1024
PIPE_AHEAD = 2
PIPE_SLOTS = 4


def _params(*semantics):
    return pltpu.CompilerParams(dimension_semantics=semantics, vmem_limit_bytes=VMEM_LIMIT_BYTES)


def _const_spec(shape):
    zeros = (0,) * len(shape)
    return pl.BlockSpec(shape, lambda *_: zeros, pipeline_mode=pl.Buffered(1))


def _mod_kernel(cond_ref, w_ref, b_ref, o_ref):
    s = jax.nn.silu(cond_ref[...]).astype(BF16)
    o_ref[0] = jnp.dot(s, w_ref[0].astype(BF16), preferred_element_type=F32) + b_ref[0]


def _modulation(cond, ada_w, ada_b, tn=1536):
    depth, d, n = ada_w.shape
    rows = cond.shape[0]
    return pl.pallas_call(
        _mod_kernel,
        grid=(depth, n // tn),
        in_specs=[
            pl.BlockSpec((rows, d), lambda i, j: (0, 0)),
            pl.BlockSpec((1, d, tn), lambda i, j: (i, 0, j)),
            pl.BlockSpec((1, 1, tn), lambda i, j: (i, 0, j)),
        ],
        out_specs=pl.BlockSpec((1, rows, tn), lambda i, j: (i, 0, j)),
        out_shape=jax.ShapeDtypeStruct((depth, rows, n), F32),
        compiler_params=_params("arbitrary", "arbitrary"),
        name="adaln_modulation",
    )(cond, ada_w, ada_b.reshape(depth, 1, n))


def _rms_modulate(x, g, shift, scale):
    y = x * lax.rsqrt(jnp.mean(x * x, axis=-1, keepdims=True) + NORM_EPS) * g
    return y * (1.0 + scale) + shift


def _head_norm_t(a, gain_b):
    n = a.shape[0] // HEAD_DIM
    tm = a.shape[1]
    a = a.reshape(n, HEAD_DIM, tm)
    a = a * lax.rsqrt(jnp.mean(a * a, axis=1, keepdims=True) + NORM_EPS) * gain_b[None]
    return a.reshape(n * HEAD_DIM, tm)


def _head_norm_rope_t(a, gain_cos, gain_sin):
    n = a.shape[0] // HEAD_DIM
    tm = a.shape[1]
    a = a.reshape(n, HEAD_DIM, tm)
    a = (a * lax.rsqrt(jnp.mean(a * a, axis=1, keepdims=True) + NORM_EPS)).astype(BF16)
    q = HEAD_DIM // 4
    swapped = jnp.concatenate([a[:, q:2 * q], a[:, 0:q], a[:, 3 * q:4 * q], a[:, 2 * q:3 * q]], axis=1)
    a = a * gain_cos[None] + swapped * gain_sin[None]
    return a.reshape(n * HEAD_DIM, tm)


def _qkv_kernel(*refs, d_model, kv_width, use_rope, chunk, sub):
    if use_rope:
        h_ref, mod_ref, g_ref, wt_ref, qc_ref, qs_ref, kc_ref, ks_ref, q_ref, k_ref, v_ref = refs
    else:
        h_ref, mod_ref, g_ref, wt_ref, gq_ref, gk_ref, q_ref, k_ref, v_ref = refs
        gq = gq_ref[...]
        gk = gk_ref[...]
    mod = mod_ref[0]
    g_eff = (g_ref[...] * (1.0 + mod[1:2])).astype(BF16)
    shift = mod[0:1].astype(BF16)
    contract_last = (((1,), (1,)), ((), ()))
    kv_chunk = min(chunk, kv_width)
    for t0 in range(0, h_ref.shape[1], sub):
        toks = slice(t0, t0 + sub)
        x = h_ref[0, toks, :]
        xn = (x * lax.rsqrt(jnp.mean(x * x, axis=-1, keepdims=True) + NORM_EPS)).astype(BF16)
        u = xn * g_eff + shift

        def proj(r0, rows):
            return lax.dot_general(wt_ref[r0:r0 + rows, :], u, contract_last,
                                   preferred_element_type=F32)

        def norm_q(a):
            if use_rope:
                return _head_norm_rope_t(a, qc_ref[:, toks], qs_ref[:, toks])
            return _head_norm_t(a, gq).astype(BF16)

        def norm_k(a):
            if use_rope:
                return _head_norm_rope_t(a, kc_ref[:, toks], ks_ref[:, toks])
            return _head_norm_t(a, gk).astype(BF16)

        for r0 in range(0, d_model, chunk):
            q_ref[0, r0:r0 + chunk, toks] = norm_q(proj(r0, chunk))
        for r0 in range(0, kv_width, kv_chunk):
            k_ref[0, toks, r0:r0 + kv_chunk] = norm_k(proj(d_model + r0, kv_chunk)).T
        for r0 in range(0, kv_width, kv_chunk):
            v_ref[0, r0:r0 + kv_chunk, toks] = proj(d_model + kv_width + r0, kv_chunk).astype(BF16)


def _qkv_project(h, mod, g, wt, gains, rope_tabs, *, kv_width, tm, sub=QKV_SUB):
    b, t, d = h.shape
    use_rope = rope_tabs is not None
    chunk = QKV_CHUNK_ROPE if use_rope else QKV_CHUNK
    in_specs = [
        pl.BlockSpec((1, tm, d), lambda i, j: (i, j, 0)),
        pl.BlockSpec((1, 6, d), lambda i, j: (i, 0, 0)),
        _const_spec((1, d)),
        _const_spec(wt.shape),
    ]
    args = [h, mod, g, wt]
    if use_rope:
        in_specs += [pl.BlockSpec((HEAD_DIM, tm), lambda i, j: (0, j))] * 4
        args += list(rope_tabs)
    else:
        in_specs += [_const_spec((HEAD_DIM, sub))] * 2
        args += list(gains)
    return pl.pallas_call(
        functools.partial(_qkv_kernel, d_model=d, kv_width=kv_width, use_rope=use_rope, chunk=chunk,
                          sub=sub),
        grid=(b, t // tm),
        in_specs=in_specs,
        out_specs=[
            pl.BlockSpec((1, d, tm), lambda i, j: (i, 0, j)),
            pl.BlockSpec((1, tm, kv_width), lambda i, j: (i, j, 0)),
            pl.BlockSpec((1, kv_width, tm), lambda i, j: (i, 0, j)),
        ],
        out_shape=[
            jax.ShapeDtypeStruct((b, d, t), BF16),
            jax.ShapeDtypeStruct((b, t, kv_width), BF16),
            jax.ShapeDtypeStruct((b, kv_width, t), BF16),
        ],
        compiler_params=_params("arbitrary", "arbitrary"),
        name="qkv_rope" if use_rope else "qkv",
    )(*args)


def _col_max(*parts):
    m = parts[0].max(axis=0, keepdims=True)
    for p in parts[1:]:
        m = jnp.maximum(m, p.max(axis=0, keepdims=True))
    return m


def _mask_rows_to_head(q_pair, head):
    row = lax.broadcasted_iota(jnp.int32, q_pair.shape, 0)
    return jnp.where((row // HEAD_DIM) == head, q_pair, jnp.zeros_like(q_pair))


def _na_build_bias(rows_ref, bias_ref):
    w = GRID_W
    kc = lax.broadcasted_iota(jnp.int32, (w, 2 * w), 0)
    qc = lax.broadcasted_iota(jnp.int32, (w, 2 * w), 1) % w
    c0 = jnp.clip(qc - NA_KW // 2, 0, w - NA_KW)
    col_valid = (kc >= c0) & (kc < c0 + NA_KW)
    plan, combos = _na_bias_plan()
    for hh in range(2):
        tiles = []
        for n in range(len(combos)):
            row = jnp.broadcast_to(rows_ref[hh, n:n + 1, :], (w, 2 * w))
            tiles.append(jnp.where(col_valid, pltpu.roll(row, 0, 1, stride=1, stride_axis=0), NEG_INF))
        for v in range(3):
            for kr in range(NA_K_ROWS):
                for half in range(NA_Q_ROWS // 2):
                    bias_ref[hh, v, kr * w:(kr + 1) * w, half * 2 * w:(half + 1) * 2 * w] = tiles[plan[v, kr, half]]


def _na_kernel(q_ref, k_ref, v_ref, kc_ref, vc_ref, rows_ref, o_ref, bias_ref, sl_ref, sc_ref, m_ref, *,
               n_blocks, grid_rows, unroll):
    @pl.when(pl.program_id(1) == 0)
    def _():
        _na_build_bias(rows_ref, bias_ref)

    ones_loc = jnp.ones((BF16_SUBLANES, NA_KB), BF16)
    ones_ctx = jnp.ones((BF16_SUBLANES, kc_ref.shape[1]), BF16)

    def window(i):
        start_row = jnp.clip(NA_Q_ROWS * i - NA_KH // 2, 0, grid_rows - NA_K_ROWS)
        return pl.multiple_of(start_row * GRID_W, NA_QB), pl.multiple_of(i * NA_QB, NA_QB)

    def scores(i, hh, slot):
        i = jnp.minimum(i, n_blocks - 1)
        k0, q0 = window(i)
        variant = jnp.where(i == 0, 0, jnp.where(i == n_blocks - 1, 2, 1))
        qm = _mask_rows_to_head(q_ref[0, :, pl.ds(q0, NA_QB)], hh)
        s_loc = jnp.dot(k_ref[0, pl.ds(k0, NA_KB), :], qm, preferred_element_type=F32) + bias_ref[hh, variant]
        s_ctx = jnp.dot(kc_ref[0], qm, preferred_element_type=F32)
        sl_ref[slot] = s_loc
        sc_ref[slot] = s_ctx
        m_ref[slot] = _col_max(s_loc, s_ctx)

    def attend(i, hh, slot):
        k0, _ = window(i)
        m = m_ref[slot]
        p_loc = jnp.exp2(sl_ref[slot] - m).astype(BF16)
        p_ctx = jnp.exp2(sc_ref[slot] - m).astype(BF16)
        rows = slice(hh * HEAD_DIM, (hh + 1) * HEAD_DIM)
        v_loc = jnp.concatenate([v_ref[0, rows, pl.ds(k0, NA_KB)], ones_loc], axis=0)
        v_ctx = jnp.concatenate([vc_ref[0, rows, :], ones_ctx], axis=0)
        o = (jnp.dot(v_loc, p_loc, preferred_element_type=F32)
             + jnp.dot(v_ctx, p_ctx, preferred_element_type=F32))
        return o[:HEAD_DIM] / o[HEAD_DIM:HEAD_DIM + 1]

    for n in range(PIPE_AHEAD):
        scores(n // 2, n % 2, n)

    single_trip = n_blocks == unroll

    def body(it, carry):
        base = it * unroll
        units = [(base + blk, hh) for blk in range(unroll + (PIPE_AHEAD + 1) // 2) for hh in range(2)]
        pair = []
        for n, (i, hh) in enumerate(units[:2 * unroll]):
            ahead = units[n + PIPE_AHEAD]
            if not (single_trip and ahead[0] >= n_blocks):
                scores(ahead[0], ahead[1], (n + PIPE_AHEAD) % PIPE_SLOTS)
            pair.append(attend(i, hh, n % PIPE_SLOTS))
            if hh == 1:
                q0 = pl.multiple_of(i * NA_QB, NA_QB)
                o_ref[0, pl.ds(q0, NA_QB), :] = jnp.concatenate(pair, axis=0).T.astype(BF16)
                pair = []
        return carry

    if single_trip:
        body(0, 0)
    else:
        lax.fori_loop(0, n_blocks // unroll, body, 0)


def _na_attention(qt, k, vt, kc, vct, bias_rows, unroll=16):
    b, d, l = qt.shape
    c = kc.shape[1]
    pairs = d // HEAD_PAIR
    grid_rows = l // GRID_W
    n_blocks = grid_rows // NA_Q_ROWS
    n_combos = bias_rows.shape[1]
    assert grid_rows >= NA_K_ROWS and n_blocks >= 3 and n_blocks % unroll == 0
    assert (2 * unroll) % PIPE_SLOTS == 0 and PIPE_SLOTS > PIPE_AHEAD
    return pl.pallas_call(
        functools.partial(_na_kernel, n_blocks=n_blocks, grid_rows=grid_rows, unroll=unroll),
        grid=(pairs, b),
        in_specs=[
            pl.BlockSpec((1, HEAD_PAIR, l), lambda p, i: (i, p, 0)),
            pl.BlockSpec((1, l, HEAD_PAIR), lambda p, i: (i, 0, p)),
            pl.BlockSpec((1, HEAD_PAIR, l), lambda p, i: (i, p, 0)),
            pl.BlockSpec((1, c, HEAD_PAIR), lambda p, i: (i, 0, p)),
            pl.BlockSpec((1, HEAD_PAIR, c), lambda p, i: (i, p, 0)),
            pl.BlockSpec((2, n_combos, 2 * GRID_W), lambda p, i: (p, 0, 0)),
        ],
        out_specs=pl.BlockSpec((1, l, HEAD_PAIR), lambda p, i: (i, 0, p)),
        out_shape=jax.ShapeDtypeStruct((b, l, d), BF16),
        scratch_shapes=[
            pltpu.VMEM((2, 3, NA_KB, NA_QB), F32),
            pltpu.VMEM((PIPE_SLOTS, NA_KB, NA_QB), F32),
            pltpu.VMEM((PIPE_SLOTS, c, NA_QB), F32),
            pltpu.VMEM((PIPE_SLOTS, 1, NA_QB), F32),
        ],
        compiler_params=_params("arbitrary", "arbitrary"),
        name="neighbourhood_attention",
    )(qt, k, vt, kc, vct, bias_rows)


NA_MASKED_ROW = 2 * NA_KH - 1


@functools.lru_cache(maxsize=None)
def _na_bias_plan():
    kr = np.arange(NA_K_ROWS)[:, None]
    qr = np.arange(NA_Q_ROWS)[None, :]
    top = NA_KH - 1
    variants = [
        (kr < NA_KH + 0 * qr, kr - qr + top),
        ((kr >= qr) & (kr < qr + NA_KH), kr - qr + top - NA_KH // 2),
        (kr >= NA_K_ROWS - NA_KH + 0 * qr, kr - qr + top - (NA_K_ROWS - NA_Q_ROWS)),
    ]
    idx = np.stack([np.where(valid, off, NA_MASKED_ROW) for valid, off in variants])
    assert idx.min() >= 0 and idx.max() <= NA_MASKED_ROW
    combos = []
    plan = np.zeros((3, NA_K_ROWS, NA_Q_ROWS // 2), np.int32)
    for v in range(3):
        for r in range(NA_K_ROWS):
            for half in range(NA_Q_ROWS // 2):
                key = (int(idx[v, r, 2 * half]), int(idx[v, r, 2 * half + 1]))
                if key not in combos:
                    combos.append(key)
                plan[v, r, half] = combos.index(key)
    return plan, tuple(combos)


def _na_bias_rows(rpb):
    h, _, n_off = rpb.shape
    w = GRID_W
    reach = NA_KW - 1
    ext = jnp.concatenate([rpb, jnp.full((h, 1, n_off), NEG_INF, rpb.dtype)], axis=1)
    rev = ext[:, :, ::-1]
    gap = jnp.full((h, (2 * w - 2 * n_off) // 2), NEG_INF, rpb.dtype)
    _, combos = _na_bias_plan()
    rows = [jnp.concatenate([rev[:, a, reach:], gap, rev[:, b, :], gap, rev[:, a, :reach]], axis=-1)
            for a, b in combos]
    out = jnp.stack(rows, axis=1)
    assert out.shape[-1] == 2 * w
    return out


def _ctx_kernel(q_ref, k_ref, v_ref, o_ref, *, pairs):
    def scores(p, hh):
        cols = slice(p * HEAD_PAIR, (p + 1) * HEAD_PAIR)
        return jnp.dot(k_ref[0, :, cols], _mask_rows_to_head(q_ref[0, cols, :], hh),
                       preferred_element_type=F32)

    heads = [(p, hh) for p in range(pairs) for hh in range(2)]
    s_next = scores(*heads[0])
    outs = []
    for n, (p, hh) in enumerate(heads):
        s = s_next
        if n + 1 < len(heads):
            s_next = scores(*heads[n + 1])
        pr = jnp.exp2(s - s.max(axis=0, keepdims=True))
        l = pr.sum(axis=0, keepdims=True)
        rows = slice(p * HEAD_PAIR + hh * HEAD_DIM, p * HEAD_PAIR + (hh + 1) * HEAD_DIM)
        outs.append(jnp.dot(v_ref[0, rows, :], pr.astype(BF16), preferred_element_type=F32) / l)
        if hh == 1:
            cols = slice(p * HEAD_PAIR, (p + 1) * HEAD_PAIR)
            o_ref[0, :, cols] = jnp.concatenate(outs, axis=0).T.astype(BF16)
            outs = []


def _ctx_attention(qct, kc, vct):
    b, d, c = qct.shape
    return pl.pallas_call(
        functools.partial(_ctx_kernel, pairs=d // HEAD_PAIR),
        grid=(b,),
        in_specs=[
            pl.BlockSpec((1, d, c), lambda i: (i, 0, 0)),
            pl.BlockSpec((1, c, d), lambda i: (i, 0, 0)),
            pl.BlockSpec((1, d, c), lambda i: (i, 0, 0)),
        ],
        out_specs=pl.BlockSpec((1, c, d), lambda i: (i, 0, 0)),
        out_shape=jax.ShapeDtypeStruct((b, c, d), BF16),
        compiler_params=_params("arbitrary"),
        name="context_attention",
    )(qct, kc, vct)


def _swa_kernel(q_ref, k_ref, v_ref, kc_ref, vc_ref, mask_ref, sink_ref, o_ref, sl_ref, sc_ref, m_ref, *,
                n_blocks, seq, unroll):
    kv_in_pair = pl.program_id(1) % 2
    ones_loc = jnp.ones((BF16_SUBLANES, SWA_KB), BF16)
    ones_ctx = jnp.ones((BF16_SUBLANES, kc_ref.shape[1]), BF16)

    def window(j):
        k0 = pl.multiple_of(jnp.clip(SWA_QB * j - SWA_WINDOW, 0, seq - SWA_KB), SWA_QB)
        return k0, pl.multiple_of(j * SWA_QB, SWA_QB)

    def scores(j, slot):
        j = jnp.minimum(j, n_blocks - 1)
        k0, q0 = window(j)
        variant = jnp.where(j == 0, 0, jnp.where(j == n_blocks - 1, 2, 1))
        qg = jnp.concatenate(
            [q_ref[0, g * HEAD_DIM:(g + 1) * HEAD_DIM, pl.ds(q0, SWA_QB)] for g in range(SWA_GROUP)],
            axis=1)
        qm = _mask_rows_to_head(jnp.concatenate([qg, qg], axis=0), kv_in_pair)
        mask = mask_ref[variant]
        s_loc = (jnp.dot(k_ref[0, pl.ds(k0, SWA_KB), :], qm, preferred_element_type=F32)
                 + jnp.concatenate([mask] * SWA_GROUP, axis=1))
        s_ctx = jnp.dot(kc_ref[0], qm, preferred_element_type=F32)
        sl_ref[slot] = s_loc
        sc_ref[slot] = s_ctx
        m_ref[slot] = jnp.maximum(_col_max(s_loc, s_ctx), sink_ref[0])

    def attend(j, slot):
        k0, q0 = window(j)
        m = m_ref[slot]
        p_loc = jnp.exp2(sl_ref[slot] - m).astype(BF16)
        p_ctx = jnp.exp2(sc_ref[slot] - m).astype(BF16)
        v_loc = jnp.concatenate([v_ref[0, :, pl.ds(k0, SWA_KB)], ones_loc], axis=0)
        v_ctx = jnp.concatenate([vc_ref[0], ones_ctx], axis=0)
        o = (jnp.dot(v_loc, p_loc, preferred_element_type=F32)
             + jnp.dot(v_ctx, p_ctx, preferred_element_type=F32))
        o = o[:HEAD_DIM] / (o[HEAD_DIM:HEAD_DIM + 1] + jnp.exp2(sink_ref[0] - m))
        o_rows = jnp.concatenate([o[:, g * SWA_QB:(g + 1) * SWA_QB] for g in range(SWA_GROUP)], axis=0)
        o_ref[0, pl.ds(q0, SWA_QB), :] = o_rows.T.astype(BF16)

    for n in range(PIPE_AHEAD):
        scores(n, n)

    def body(it, carry):
        base = it * unroll
        for n in range(unroll):
            scores(base + n + PIPE_AHEAD, (n + PIPE_AHEAD) % PIPE_SLOTS)
            attend(base + n, n % PIPE_SLOTS)
        return carry

    lax.fori_loop(0, n_blocks // unroll, body, 0)


def _swa_mask_tables():
    kr = np.arange(SWA_KB)[:, None]
    qc = np.arange(SWA_QB)[None, :]
    rel = [kr - qc, kr - SWA_WINDOW - qc, kr - (SWA_KB - SWA_QB) - qc]
    return np.stack([np.where(np.abs(r) <= SWA_WINDOW, 0.0, NEG_INF) for r in rel]).astype(np.float32)


def _swa_attention(qt, k, vt, kc, vct, sink_rows, unroll=16):
    b, d, l = qt.shape
    c = kc.shape[1]
    kv = k.shape[2] // HEAD_DIM
    gw = SWA_GROUP * HEAD_DIM
    n_blocks = l // SWA_QB
    assert l >= SWA_KB and n_blocks >= 3 and n_blocks % unroll == 0
    assert unroll % PIPE_SLOTS == 0 and PIPE_SLOTS > PIPE_AHEAD
    return pl.pallas_call(
        functools.partial(_swa_kernel, n_blocks=n_blocks, seq=l, unroll=unroll),
        grid=(b, kv),
        in_specs=[
            pl.BlockSpec((1, gw, l), lambda i, n: (i, n, 0)),
            pl.BlockSpec((1, l, HEAD_PAIR), lambda i, n: (i, 0, n // 2)),
            pl.BlockSpec((1, HEAD_DIM, l), lambda i, n: (i, n, 0)),
            pl.BlockSpec((1, c, HEAD_PAIR), lambda i, n: (i, 0, n // 2)),
            pl.BlockSpec((1, HEAD_DIM, c), lambda i, n: (i, n, 0)),
            _const_spec((3, SWA_KB, SWA_QB)),
            pl.BlockSpec((1, 1, SWA_GROUP * SWA_QB), lambda i, n: (n, 0, 0)),
        ],
        out_specs=pl.BlockSpec((1, l, gw), lambda i, n: (i, 0, n)),
        out_shape=jax.ShapeDtypeStruct((b, l, d), BF16),
        scratch_shapes=[
            pltpu.VMEM((PIPE_SLOTS, SWA_KB, SWA_GROUP * SWA_QB), F32),
            pltpu.VMEM((PIPE_SLOTS, c, SWA_GROUP * SWA_QB), F32),
            pltpu.VMEM((PIPE_SLOTS, 1, SWA_GROUP * SWA_QB), F32),
        ],
        compiler_params=_params("arbitrary", "arbitrary"),
        name="sliding_window_attention",
    )(qt, k, vt, kc, vct, jnp.asarray(_swa_mask_tables()), sink_rows)


def _post_kernel(a_ref, h_ref, mod_ref, g_ref, wo_ref, w1_ref, w2_ref, o_ref, *, ff_chunk):
    mod = mod_ref[0]
    y = jnp.dot(a_ref[0], wo_ref[...], preferred_element_type=F32)
    h1 = h_ref[0] + mod[2:3] * y
    u = _rms_modulate(h1, g_ref[...], mod[3:4], mod[4:5]).astype(BF16)
    d_ff = w1_ref.shape[1]
    acc = jnp.zeros_like(h1)
    for c0 in range(0, d_ff, ff_chunk):
        t = jnp.dot(u, w1_ref[:, c0:c0 + ff_chunk], preferred_element_type=F32)
        r = jnp.square(jnp.maximum(t, 0.0)).astype(BF16)
        acc = acc + jnp.dot(r, w2_ref[c0:c0 + ff_chunk, :], preferred_element_type=F32)
    o_ref[0] = h1 + mod[5:6] * acc


def _post_block(a, h, mod, g, wo, w1, w2, *, tm, ff_chunk=1024):
    b, t, d = h.shape
    return pl.pallas_call(
        functools.partial(_post_kernel, ff_chunk=ff_chunk),
        grid=(b, t // tm),
        in_specs=[
            pl.BlockSpec((1, tm, d), lambda i, j: (i, j, 0)),
            pl.BlockSpec((1, tm, d), lambda i, j: (i, j, 0)),
            pl.BlockSpec((1, 6, d), lambda i, j: (i, 0, 0)),
            _const_spec((1, d)),
            _const_spec(wo.shape),
            _const_spec(w1.shape),
            _const_spec(w2.shape),
        ],
        out_specs=pl.BlockSpec((1, tm, d), lambda i, j: (i, j, 0)),
        out_shape=jax.ShapeDtypeStruct((b, t, d), F32),
        compiler_params=_params("arbitrary", "arbitrary"),
        name="outproj_mlp",
    )(a, h, mod, g, wo, w1, w2)


def _rope_tables_t(l):
    t = np.arange(l)
    n_freq = HEAD_DIM // 4
    inv = ROPE_BASE ** (-np.arange(n_freq, dtype=np.float64) / n_freq)
    ang = np.stack([(t // GRID_W)[:, None] * inv, (t % GRID_W)[:, None] * inv], axis=1)
    cos = np.transpose(np.cos(ang), (1, 2, 0))
    sin = np.transpose(np.sin(ang), (1, 2, 0))
    cos_t = np.stack([cos, cos], axis=1).reshape(HEAD_DIM, l)
    sin_t = np.stack([-sin, sin], axis=1).reshape(HEAD_DIM, l)
    return jnp.asarray(cos_t, F32), jnp.asarray(sin_t, F32)


def _swap_rotary_halves(v):
    q = HEAD_DIM // 4
    return jnp.concatenate([v[q:2 * q], v[0:q], v[3 * q:4 * q], v[2 * q:3 * q]], axis=0)


def _gain_rope_tables(gain, cos_t, sin_t):
    gain = gain.astype(F32)
    return ((gain[:, None] * cos_t).astype(BF16), (_swap_rotary_halves(gain)[:, None] * sin_t).astype(BF16))


def _gain_cols(gain, scale=1.0):
    return jnp.broadcast_to((gain.astype(F32) * scale)[:, None], (HEAD_DIM, QKV_SUB))


def kernel(x, c, ctx, c_ctx, ada_w, ada_b, g_mix, g_mlp, mlp_w1, mlp_w2, na_wqkv, na_q_gain, na_k_gain,
           na_rpb, na_wo, swa_wqkv, swa_q_gain, swa_k_gain, swa_sink, swa_wo):
    b, l, d = x.shape
    n_ctx = ctx.shape[1]
    tm = TM_POST
    tmc = n_ctx
    scale = HEAD_DIM ** -0.5 * LOG2E

    cond =jnp.concatenate([c, c_ctx[None], jnp.zeros((8 - b - 1, d), F32)], axis=0)
    mods = _modulation(cond, ada_w, ada_b)

    def mods_of(i):
        lat = mods[i, :b].reshape(b, 6, d)
        cx = jnp.broadcast_to(mods[i, b].reshape(1, 6, d), (b, 6, d))
        return lat, cx

    def row(v):
        return v.reshape(1, d)

    mod_lat, mod_ctx = mods_of(0)
    wt = na_wqkv[0].T.astype(BF16)
    gq, gk = na_q_gain[0], na_k_gain[0]
    gains = (_gain_cols(gq, scale), _gain_cols(gk))
    qt, k, vt = _qkv_project(x, mod_lat, row(g_mix[0]), wt, gains, None, kv_width=d, tm=TM_QKV)
    qct, kc, vct = _qkv_project(ctx, mod_ctx, row(g_mix[0]), wt, gains, None, kv_width=d, tm=tmc)
    a_lat = _na_attention(qt, k, vt, kc, vct, _na_bias_rows(na_rpb[0] * LOG2E))
    a_ctx = _ctx_attention(qct, kc, vct)
    wo = na_wo[0].astype(BF16)
    w1 = mlp_w1[0].astype(BF16)
    w2 = mlp_w2[0].astype(BF16)
    h_lat = _post_block(a_lat, x, mod_lat, row(g_mlp[0]), wo, w1, w2, tm=tm)
    h_ctx = _post_block(a_ctx, ctx, mod_ctx, row(g_mlp[0]), wo, w1, w2, tm=tmc)

    mod_lat, mod_ctx = mods_of(1)
    kvw = SWA_KV_HEADS * HEAD_DIM
    wt = swa_wqkv[0].T.astype(BF16)
    gq, gk = swa_q_gain[0], swa_k_gain[0]
    cos_t, sin_t = _rope_tables_t(l)
    rope_tabs = (_gain_rope_tables(gq * scale, cos_t, sin_t) + _gain_rope_tables(gk, cos_t, sin_t))
    qt, k, vt = _qkv_project(h_lat, mod_lat, row(g_mix[1]), wt, None, rope_tabs, kv_width=kvw, tm=TM_QKV)
    gains = (_gain_cols(gq, scale), _gain_cols(gk))
    _, kc, vct = _qkv_project(h_ctx, mod_ctx, row(g_mix[1]), wt, gains, None, kv_width=kvw, tm=tmc)
    sink_rows = jnp.repeat(swa_sink[0].astype(F32) * LOG2E, SWA_QB).reshape(SWA_KV_HEADS, 1, SWA_GROUP * SWA_QB)
    a_lat = _swa_attention(qt, k, vt, kc, vct, sink_rows)
    return _post_block(a_lat, h_lat, mod_lat, row(g_mlp[1]), swa_wo[0].astype(BF16),
                       mlp_w1[1].astype(BF16), mlp_w2[1].astype(BF16), tm=tm)
```

```python
import functools

import numpy as np
import jax
import jax.numpy as jnp
from jax import lax
from jax.experimental import pallas as pl
from jax.experimental.pallas import tpu as pltpu

GRID_W = 64
N_HEADS = 16
HEAD_DIM = 64
NA_KH = 8
NA_KW = 16
SWA_KV_HEADS = 4
SWA_GROUP = N_HEADS // SWA_KV_HEADS
SWA_WINDOW = 128
ROPE_BASE = 10000.0
NORM_EPS = 1e-6
NEG_INF = -1e30
LOG2E = float(np.log2(np.e))

F32 = jnp.float32
BF16 = jnp.bfloat16

V7X_VMEM_BYTES = 64 * 1024 * 1024
VMEM_LIMIT_BYTES = V7X_VMEM_BYTES - 8 * 1024 * 1024

BF16_SUBLANES = 16
HEAD_PAIR = 2 * HEAD_DIM
NA_Q_ROWS = 4
NA_K_ROWS = NA_Q_ROWS + NA_KH
NA_QB = NA_Q_ROWS * GRID_W
NA_KB = NA_K_ROWS * GRID_W
SWA_QB = 128
SWA_KB = SWA_QB + 2 * SWA_WINDOW
TM_QKV = 1024
TM_POST = 1024
QKV_SUB = 256
QKV_CHUNK = 256
QKV_CHUNK_ROPE = 1024
PIPE_AHEAD = 2
PIPE_SLOTS = 4


def _params(*semantics):
    return pltpu.CompilerParams(dimension_semantics=semantics, vmem_limit_bytes=VMEM_LIMIT_BYTES)


def _const_spec(shape):
    zeros = (0,) * len(shape)
    return pl.BlockSpec(shape, lambda *_: zeros, pipeline_mode=pl.Buffered(1))


def _mod_kernel(cond_ref, w_ref, b_ref, o_ref):
    s = jax.nn.silu(cond_ref[...]).astype(BF16)
    o_ref[0] = jnp.dot(s, w_ref[0].astype(BF16), preferred_element_type=F32) + b_ref[0]


def _modulation(cond, ada_w, ada_b, tn=1536):
    depth, d, n = ada_w.shape
    rows = cond.shape[0]
    return pl.pallas_call(
        _mod_kernel,
        grid=(depth, n // tn),
        in_specs=[
            pl.BlockSpec((rows, d), lambda i, j: (0, 0)),
            pl.BlockSpec((1, d, tn), lambda i, j: (i, 0, j)),
            pl.BlockSpec((1, 1, tn), lambda i, j: (i, 0, j)),
        ],
        out_specs=pl.BlockSpec((1, rows, tn), lambda i, j: (i, 0, j)),
        out_shape=jax.ShapeDtypeStruct((depth, rows, n), F32),
        compiler_params=_params("arbitrary", "arbitrary"),
        name="adaln_modulation",
    )(cond, ada_w, ada_b.reshape(depth, 1, n))


def _rms_modulate(x, g, shift, scale):
    y = x * lax.rsqrt(jnp.mean(x * x, axis=-1, keepdims=True) + NORM_EPS) * g
    return y * (1.0 + scale) + shift


def _head_norm_t(a, gain_b):
    n = a.shape[0] // HEAD_DIM
    tm = a.shape[1]
    a = a.reshape(n, HEAD_DIM, tm)
    a = (a * lax.rsqrt(jnp.mean(a * a, axis=1, keepdims=True) + NORM_EPS)).astype(BF16) * gain_b[None]
    return a.reshape(n * HEAD_DIM, tm)


def _head_norm_rope_t(a, gain_cos, gain_sin):
    n = a.shape[0] // HEAD_DIM
    tm = a.shape[1]
    a = a.reshape(n, HEAD_DIM, tm)
    a = (a * lax.rsqrt(jnp.mean(a * a, axis=1, keepdims=True) + NORM_EPS)).astype(BF16)
    q = HEAD_DIM // 4
    swapped = jnp.concatenate([a[:, q:2 * q], a[:, 0:q], a[:, 3 * q:4 * q], a[:, 2 * q:3 * q]], axis=1)
    a = a * gain_cos[None] + swapped * gain_sin[None]
    return a.reshape(n * HEAD_DIM, tm)


def _qkv_kernel(*refs, d_model, kv_width, use_rope, chunk, sub):
    if use_rope:
        h_ref, mod_ref, g_ref, wt_ref, qc_ref, qs_ref, kc_ref, ks_ref, q_ref, k_ref, v_ref = refs
    else:
        h_ref, mod_ref, g_ref, wt_ref, gq_ref, gk_ref, q_ref, k_ref, v_ref = refs
        gq = gq_ref[...].astype(BF16)
        gk = gk_ref[...].astype(BF16)
    mod = mod_ref[0]
    g_eff = (g_ref[...] * (1.0 + mod[1:2])).astype(BF16)
    shift = mod[0:1].astype(BF16)
    contract_last = (((1,), (1,)), ((), ()))
    kv_chunk = min(chunk, kv_width)
    for t0 in range(0, h_ref.shape[1], sub):
        toks = slice(t0, t0 + sub)
        x = h_ref[0, toks, :]
        xn = (x * lax.rsqrt(jnp.mean(x * x, axis=-1, keepdims=True) + NORM_EPS)).astype(BF16)
        u = xn * g_eff + shift

        def proj(r0, rows):
            return lax.dot_general(wt_ref[r0:r0 + rows, :], u, contract_last,
                                   preferred_element_type=F32)

        def norm_q(a):
            if use_rope:
                return _head_norm_rope_t(a, qc_ref[:, toks], qs_ref[:, toks])
            return _head_norm_t(a, gq)

        def norm_k(a):
            if use_rope:
                return _head_norm_rope_t(a, kc_ref[:, toks], ks_ref[:, toks])
            return _head_norm_t(a, gk)

        for r0 in range(0, d_model, chunk):
            q_ref[0, r0:r0 + chunk, toks] = norm_q(proj(r0, chunk))
        for r0 in range(0, kv_width, kv_chunk):
            k_ref[0, toks, r0:r0 + kv_chunk] = norm_k(proj(d_model + r0, kv_chunk)).T
        for r0 in range(0, kv_width, kv_chunk):
            v_ref[0, r0:r0 + kv_chunk, toks] = proj(d_model + kv_width + r0, kv_chunk).astype(BF16)


def _qkv_project(h, mod, g, wt, gains, rope_tabs, *, kv_width, tm, sub=QKV_SUB):
    b, t, d = h.shape
    use_rope = rope_tabs is not None
    chunk = QKV_CHUNK_ROPE if use_rope else QKV_CHUNK
    in_specs = [
        pl.BlockSpec((1, tm, d), lambda i, j: (i, j, 0)),
        pl.BlockSpec((1, 6, d), lambda i, j: (i, 0, 0)),
        _const_spec((1, d)),
        _const_spec(wt.shape),
    ]
    args = [h, mod, g, wt]
    if use_rope:
        in_specs += [pl.BlockSpec((HEAD_DIM, tm), lambda i, j: (0, j))] * 4
        args += list(rope_tabs)
    else:
        in_specs += [_const_spec((HEAD_DIM, sub))] * 2
        args += list(gains)
    return pl.pallas_call(
        functools.partial(_qkv_kernel, d_model=d, kv_width=kv_width, use_rope=use_rope, chunk=chunk,
                          sub=sub),
        grid=(b, t // tm),
        in_specs=in_specs,
        out_specs=[
            pl.BlockSpec((1, d, tm), lambda i, j: (i, 0, j)),
            pl.BlockSpec((1, tm, kv_width), lambda i, j: (i, j, 0)),
            pl.BlockSpec((1, kv_width, tm), lambda i, j: (i, 0, j)),
        ],
        out_shape=[
            jax.ShapeDtypeStruct((b, d, t), BF16),
            jax.ShapeDtypeStruct((b, t, kv_width), BF16),
            jax.ShapeDtypeStruct((b, kv_width, t), BF16),
        ],
        compiler_params=_params("arbitrary", "arbitrary"),
        name="qkv_rope" if use_rope else "qkv",
    )(*args)


def _col_max(*parts):
    m = parts[0].max(axis=0, keepdims=True)
    for p in parts[1:]:
        m = jnp.maximum(m, p.max(axis=0, keepdims=True))
    return m


def _mask_rows_to_head(q_pair, head):
    row = lax.broadcasted_iota(jnp.int32, q_pair.shape, 0)
    return jnp.where((row // HEAD_DIM) == head, q_pair, jnp.zeros_like(q_pair))


def _na_build_bias(rows_ref, bias_ref):
    w = GRID_W
    kc = lax.broadcasted_iota(jnp.int32, (w, 2 * w), 0)
    qc = lax.broadcasted_iota(jnp.int32, (w, 2 * w), 1) % w
    c0 = jnp.clip(qc - NA_KW // 2, 0, w - NA_KW)
    col_valid = (kc >= c0) & (kc < c0 + NA_KW)
    plan, combos = _na_bias_plan()
    for hh in range(2):
        tiles = []
        for n in range(len(combos)):
            row = jnp.broadcast_to(rows_ref[hh, n:n + 1, :], (w, 2 * w))
            tiles.append(jnp.where(col_valid, pltpu.roll(row, 0, 1, stride=1, stride_axis=0), NEG_INF))
        for v in range(3):
            for kr in range(NA_K_ROWS):
                for half in range(NA_Q_ROWS // 2):
                    bias_ref[hh, v, kr * w:(kr + 1) * w, half * 2 * w:(half + 1) * 2 * w] = tiles[plan[v, kr, half]]


def _na_kernel(q_ref, k_ref, v_ref, kc_ref, vc_ref, rows_ref, o_ref, bias_ref, sl_ref, sc_ref, m_ref, *,
               n_blocks, grid_rows, unroll):
    @pl.when(pl.program_id(1) == 0)
    def _():
        _na_build_bias(rows_ref, bias_ref)

    ones_loc = jnp.ones((BF16_SUBLANES, NA_KB), BF16)
    ones_ctx = jnp.ones((BF16_SUBLANES, kc_ref.shape[1]), BF16)

    def window(i):
        start_row = jnp.clip(NA_Q_ROWS * i - NA_KH // 2, 0, grid_rows - NA_K_ROWS)
        return pl.multiple_of(start_row * GRID_W, NA_QB), pl.multiple_of(i * NA_QB, NA_QB)

    def scores(i, hh, slot):
        i = jnp.minimum(i, n_blocks - 1)
        k0, q0 = window(i)
        variant = jnp.where(i == 0, 0, jnp.where(i == n_blocks - 1, 2, 1))
        qm = _mask_rows_to_head(q_ref[0, :, pl.ds(q0, NA_QB)], hh)
        s_loc = jnp.dot(k_ref[0, pl.ds(k0, NA_KB), :], qm, preferred_element_type=F32) + bias_ref[hh, variant]
        s_ctx = jnp.dot(kc_ref[0], qm, preferred_element_type=F32)
        sl_ref[slot] = s_loc
        sc_ref[slot] = s_ctx
        m_ref[slot] = _col_max(s_loc, s_ctx)

    def attend(i, hh, slot):
        k0, _ = window(i)
        m = m_ref[slot]
        p_loc = jnp.exp2(sl_ref[slot] - m).astype(BF16)
        p_ctx = jnp.exp2(sc_ref[slot] - m).astype(BF16)
        rows = slice(hh * HEAD_DIM, (hh + 1) * HEAD_DIM)
        v_loc = jnp.concatenate([v_ref[0, rows, pl.ds(k0, NA_KB)], ones_loc], axis=0)
        v_ctx = jnp.concatenate([vc_ref[0, rows, :], ones_ctx], axis=0)
        o = (jnp.dot(v_loc, p_loc, preferred_element_type=F32)
             + jnp.dot(v_ctx, p_ctx, preferred_element_type=F32))
        return o[:HEAD_DIM] / o[HEAD_DIM:HEAD_DIM + 1]

    for n in range(PIPE_AHEAD):
        scores(n // 2, n % 2, n)

    single_trip = n_blocks == unroll

    def body(it, carry):
        base = it * unroll
        units = [(base + blk, hh) for blk in range(unroll + (PIPE_AHEAD + 1) // 2) for hh in range(2)]
        pair = []
        for n, (i, hh) in enumerate(units[:2 * unroll]):
            ahead = units[n + PIPE_AHEAD]
            if not (single_trip and ahead[0] >= n_blocks):
                scores(ahead[0], ahead[1], (n + PIPE_AHEAD) % PIPE_SLOTS)
            pair.append(attend(i, hh, n % PIPE_SLOTS))
            if hh == 1:
                q0 = pl.multiple_of(i * NA_QB, NA_QB)
                o_ref[0, pl.ds(q0, NA_QB), :] = jnp.concatenate(pair, axis=0).T.astype(BF16)
                pair = []
        return carry

    if single_trip:
        body(0, 0)
    else:
        lax.fori_loop(0, n_blocks // unroll, body, 0)


def _na_attention(qt, k, vt, kc, vct, bias_rows, unroll=16):
    b, d, l = qt.shape
    c = kc.shape[1]
    pairs = d // HEAD_PAIR
    grid_rows = l // GRID_W
    n_blocks = grid_rows // NA_Q_ROWS
    n_combos = bias_rows.shape[1]
    assert grid_rows >= NA_K_ROWS and n_blocks >= 3 and n_blocks % unroll == 0
    assert (2 * unroll) % PIPE_SLOTS == 0 and PIPE_SLOTS > PIPE_AHEAD
    return pl.pallas_call(
        functools.partial(_na_kernel, n_blocks=n_blocks, grid_rows=grid_rows, unroll=unroll),
        grid=(pairs, b),
        in_specs=[
            pl.BlockSpec((1, HEAD_PAIR, l), lambda p, i: (i, p, 0)),
            pl.BlockSpec((1, l, HEAD_PAIR), lambda p, i: (i, 0, p)),
            pl.BlockSpec((1, HEAD_PAIR, l), lambda p, i: (i, p, 0)),
            pl.BlockSpec((1, c, HEAD_PAIR), lambda p, i: (i, 0, p)),
            pl.BlockSpec((1, HEAD_PAIR, c), lambda p, i: (i, p, 0)),
            pl.BlockSpec((2, n_combos, 2 * GRID_W), lambda p, i: (p, 0, 0)),
        ],
        out_specs=pl.BlockSpec((1, l, HEAD_PAIR), lambda p, i: (i, 0, p)),
        out_shape=jax.ShapeDtypeStruct((b, l, d), BF16),
        scratch_shapes=[
            pltpu.VMEM((2, 3, NA_KB, NA_QB), F32),
            pltpu.VMEM((PIPE_SLOTS, NA_KB, NA_QB), F32),
            pltpu.VMEM((PIPE_SLOTS, c, NA_QB), F32),
            pltpu.VMEM((PIPE_SLOTS, 1, NA_QB), F32),
        ],
        compiler_params=_params("arbitrary", "arbitrary"),
        name="neighbourhood_attention",
    )(qt, k, vt, kc, vct, bias_rows)


NA_MASKED_ROW = 2 * NA_KH - 1


@functools.lru_cache(maxsize=None)
def _na_bias_plan():
    kr = np.arange(NA_K_ROWS)[:, None]
    qr = np.arange(NA_Q_ROWS)[None, :]
    top = NA_KH - 1
    variants = [
        (kr < NA_KH + 0 * qr, kr - qr + top),
        ((kr >= qr) & (kr < qr + NA_KH), kr - qr + top - NA_KH // 2),
        (kr >= NA_K_ROWS - NA_KH + 0 * qr, kr - qr + top - (NA_K_ROWS - NA_Q_ROWS)),
    ]
    idx = np.stack([np.where(valid, off, NA_MASKED_ROW) for valid, off in variants])
    assert idx.min() >= 0 and idx.max() <= NA_MASKED_ROW
    combos = []
    plan = np.zeros((3, NA_K_ROWS, NA_Q_ROWS // 2), np.int32)
    for v in range(3):
        for r in range(NA_K_ROWS):
            for half in range(NA_Q_ROWS // 2):
                key = (int(idx[v, r, 2 * half]), int(idx[v, r, 2 * half + 1]))
                if key not in combos:
                    combos.append(key)
                plan[v, r, half] = combos.index(key)
    return plan, tuple(combos)


def _na_bias_rows(rpb):
    h, n_rows, n_off = rpb.shape
    w = GRID_W
    reach = NA_KW - 1
    _, combos = _na_bias_plan()
    masked = n_rows * n_off
    lane = np.arange(2 * w)
    first = np.where(lane <= reach, reach - lane, np.where(lane >= 2 * w - reach, reach + 2 * w - lane, -1))
    second = np.where(np.abs(lane - w) <= reach, reach + w - lane, -1)
    idx = np.full((len(combos), 2 * w), masked, np.int32)
    for n, (a, b) in enumerate(combos):
        if a != NA_MASKED_ROW:
            idx[n] = np.where(first >= 0, a * n_off + first, idx[n])
        if b != NA_MASKED_ROW:
            idx[n] = np.where(second >= 0, b * n_off + second, idx[n])
    flat = jnp.concatenate([rpb.reshape(h, masked), jnp.full((h, 1), NEG_INF, rpb.dtype)], axis=1)
    return flat[:, idx]


def _ctx_kernel(q_ref, k_ref, v_ref, o_ref, *, pairs):
    def scores(p, hh):
        cols = slice(p * HEAD_PAIR, (p + 1) * HEAD_PAIR)
        return jnp.dot(k_ref[0, :, cols], _mask_rows_to_head(q_ref[0, cols, :], hh),
                       preferred_element_type=F32)

    heads = [(p, hh) for p in range(pairs) for hh in range(2)]
    s_next = scores(*heads[0])
    outs = []
    for n, (p, hh) in enumerate(heads):
        s = s_next
        if n + 1 < len(heads):
            s_next = scores(*heads[n + 1])
        pr = jnp.exp2(s - s.max(axis=0, keepdims=True))
        l = pr.sum(axis=0, keepdims=True)
        rows = slice(p * HEAD_PAIR + hh * HEAD_DIM, p * HEAD_PAIR + (hh + 1) * HEAD_DIM)
        outs.append(jnp.dot(v_ref[0, rows, :], pr.astype(BF16), preferred_element_type=F32) / l)
        if hh == 1:
            cols = slice(p * HEAD_PAIR, (p + 1) * HEAD_PAIR)
            o_ref[0, :, cols] = jnp.concatenate(outs, axis=0).T.astype(BF16)
            outs = []


def _ctx_attention(qct, kc, vct):
    b, d, c = qct.shape
    return pl.pallas_call(
        functools.partial(_ctx_kernel, pairs=d // HEAD_PAIR),
        grid=(b,),
        in_specs=[
            pl.BlockSpec((1, d, c), lambda i: (i, 0, 0)),
            pl.BlockSpec((1, c, d), lambda i: (i, 0, 0)),
            pl.BlockSpec((1, d, c), lambda i: (i, 0, 0)),
        ],
        out_specs=pl.BlockSpec((1, c, d), lambda i: (i, 0, 0)),
        out_shape=jax.ShapeDtypeStruct((b, c, d), BF16),
        compiler_params=_params("arbitrary"),
        name="context_attention",
    )(qct, kc, vct)


def _swa_kernel(q_ref, k_ref, v_ref, kc_ref, vc_ref, mask_ref, sink_ref, o_ref, sl_ref, sc_ref, m_ref, *,
                n_blocks, seq, unroll):
    kv_in_pair = pl.program_id(1) % 2
    ones_loc = jnp.ones((BF16_SUBLANES, SWA_KB), BF16)
    ones_ctx = jnp.ones((BF16_SUBLANES, kc_ref.shape[1]), BF16)

    def window(j):
        k0 = pl.multiple_of(jnp.clip(SWA_QB * j - SWA_WINDOW, 0, seq - SWA_KB), SWA_QB)
        return k0, pl.multiple_of(j * SWA_QB, SWA_QB)

    def scores(j, slot):
        j = jnp.minimum(j, n_blocks - 1)
        k0, q0 = window(j)
        variant = jnp.where(j == 0, 0, jnp.where(j == n_blocks - 1, 2, 1))
        qg = jnp.concatenate(
            [q_ref[0, g * HEAD_DIM:(g + 1) * HEAD_DIM, pl.ds(q0, SWA_QB)] for g in range(SWA_GROUP)],
            axis=1)
        qm = _mask_rows_to_head(jnp.concatenate([qg, qg], axis=0), kv_in_pair)
        mask = mask_ref[variant]
        s_loc = (jnp.dot(k_ref[0, pl.ds(k0, SWA_KB), :], qm, preferred_element_type=F32)
                 + jnp.concatenate([mask] * SWA_GROUP, axis=1))
        s_ctx = jnp.dot(kc_ref[0], qm, preferred_element_type=F32)
        sl_ref[slot] = s_loc
        sc_ref[slot] = s_ctx
        m_ref[slot] = jnp.maximum(_col_max(s_loc, s_ctx), sink_ref[0])

    def attend(j, slot):
        k0, q0 = window(j)
        m = m_ref[slot]
        p_loc = jnp.exp2(sl_ref[slot] - m).astype(BF16)
        p_ctx = jnp.exp2(sc_ref[slot] - m).astype(BF16)
        v_loc = jnp.concatenate([v_ref[0, :, pl.ds(k0, SWA_KB)], ones_loc], axis=0)
        v_ctx = jnp.concatenate([vc_ref[0], ones_ctx], axis=0)
        o = (jnp.dot(v_loc, p_loc, preferred_element_type=F32)
             + jnp.dot(v_ctx, p_ctx, preferred_element_type=F32))
        o = o[:HEAD_DIM] / (o[HEAD_DIM:HEAD_DIM + 1] + jnp.exp2(sink_ref[0] - m))
        o_rows = jnp.concatenate([o[:, g * SWA_QB:(g + 1) * SWA_QB] for g in range(SWA_GROUP)], axis=0)
        o_ref[0, pl.ds(q0, SWA_QB), :] = o_rows.T.astype(BF16)

    for n in range(PIPE_AHEAD):
        scores(n, n)

    def body(it, carry):
        base = it * unroll
        for n in range(unroll):
            scores(base + n + PIPE_AHEAD, (n + PIPE_AHEAD) % PIPE_SLOTS)
            attend(base + n, n % PIPE_SLOTS)
        return carry

    lax.fori_loop(0, n_blocks // unroll, body, 0)


def _swa_mask_tables():
    kr = np.arange(SWA_KB)[:, None]
    qc = np.arange(SWA_QB)[None, :]
    rel = [kr - qc, kr - SWA_WINDOW - qc, kr - (SWA_KB - SWA_QB) - qc]
    return np.stack([np.where(np.abs(r) <= SWA_WINDOW, 0.0, NEG_INF) for r in rel]).astype(np.float32)


def _swa_attention(qt, k, vt, kc, vct, sink_rows, unroll=16):
    b, d, l = qt.shape
    c = kc.shape[1]
    kv = k.shape[2] // HEAD_DIM
    gw = SWA_GROUP * HEAD_DIM
    n_blocks = l // SWA_QB
    assert l >= SWA_KB and n_blocks >= 3 and n_blocks % unroll == 0
    assert unroll % PIPE_SLOTS == 0 and PIPE_SLOTS > PIPE_AHEAD
    return pl.pallas_call(
        functools.partial(_swa_kernel, n_blocks=n_blocks, seq=l, unroll=unroll),
        grid=(b, kv),
        in_specs=[
            pl.BlockSpec((1, gw, l), lambda i, n: (i, n, 0)),
            pl.BlockSpec((1, l, HEAD_PAIR), lambda i, n: (i, 0, n // 2)),
            pl.BlockSpec((1, HEAD_DIM, l), lambda i, n: (i, n, 0)),
            pl.BlockSpec((1, c, HEAD_PAIR), lambda i, n: (i, 0, n // 2)),
            pl.BlockSpec((1, HEAD_DIM, c), lambda i, n: (i, n, 0)),
            _const_spec((3, SWA_KB, SWA_QB)),
            pl.BlockSpec((1, 1, SWA_GROUP * SWA_QB), lambda i, n: (n, 0, 0)),
        ],
        out_specs=pl.BlockSpec((1, l, gw), lambda i, n: (i, 0, n)),
        out_shape=jax.ShapeDtypeStruct((b, l, d), BF16),
        scratch_shapes=[
            pltpu.VMEM((PIPE_SLOTS, SWA_KB, SWA_GROUP * SWA_QB), F32),
            pltpu.VMEM((PIPE_SLOTS, c, SWA_GROUP * SWA_QB), F32),
            pltpu.VMEM((PIPE_SLOTS, 1, SWA_GROUP * SWA_QB), F32),
        ],
        compiler_params=_params("arbitrary", "arbitrary"),
        name="sliding_window_attention",
    )(qt, k, vt, kc, vct, jnp.asarray(_swa_mask_tables()), sink_rows)


def _post_kernel(a_ref, h_ref, mod_ref, g_ref, wo_ref, w1_ref, w2_ref, o_ref, *, ff_chunk):
    mod = mod_ref[0]
    y = jnp.dot(a_ref[0], wo_ref[...], preferred_element_type=F32)
    h1 = h_ref[0] + mod[2:3] * y
    u = _rms_modulate(h1, g_ref[...], mod[3:4], mod[4:5]).astype(BF16)
    d_ff = w1_ref.shape[1]
    acc = jnp.zeros_like(h1)
    for c0 in range(0, d_ff, ff_chunk):
        t = jnp.dot(u, w1_ref[:, c0:c0 + ff_chunk], preferred_element_type=F32)
        r = jnp.square(jnp.maximum(t, 0.0)).astype(BF16)
        acc = acc + jnp.dot(r, w2_ref[c0:c0 + ff_chunk, :], preferred_element_type=F32)
    o_ref[0] = h1 + mod[5:6] * acc


def _post_block(a, h, mod, g, wo, w1, w2, *, tm, ff_chunk=1024):
    b, t, d = h.shape
    return pl.pallas_call(
        functools.partial(_post_kernel, ff_chunk=ff_chunk),
        grid=(b, t // tm),
        in_specs=[
            pl.BlockSpec((1, tm, d), lambda i, j: (i, j, 0)),
            pl.BlockSpec((1, tm, d), lambda i, j: (i, j, 0)),
            pl.BlockSpec((1, 6, d), lambda i, j: (i, 0, 0)),
            _const_spec((1, d)),
            _const_spec(wo.shape),
            _const_spec(w1.shape),
            _const_spec(w2.shape),
        ],
        out_specs=pl.BlockSpec((1, tm, d), lambda i, j: (i, j, 0)),
        out_shape=jax.ShapeDtypeStruct((b, t, d), F32),
        compiler_params=_params("arbitrary", "arbitrary"),
        name="outproj_mlp",
    )(a, h, mod, g, wo, w1, w2)


def _rope_tables_t(l):
    t = np.arange(l)
    n_freq = HEAD_DIM // 4
    inv = ROPE_BASE ** (-np.arange(n_freq, dtype=np.float64) / n_freq)
    ang = np.stack([(t // GRID_W)[:, None] * inv, (t % GRID_W)[:, None] * inv], axis=1)
    cos = np.transpose(np.cos(ang), (1, 2, 0))
    sin = np.transpose(np.sin(ang), (1, 2, 0))
    cos_t = np.stack([cos, cos], axis=1).reshape(HEAD_DIM, l)
    sin_t = np.stack([-sin, sin], axis=1).reshape(HEAD_DIM, l)
    return jnp.asarray(cos_t, F32), jnp.asarray(sin_t, F32)


def _swap_rotary_halves(v):
    q = HEAD_DIM // 4
    return jnp.concatenate([v[q:2 * q], v[0:q], v[3 * q:4 * q], v[2 * q:3 * q]], axis=0)


def _gain_rope_tables(gain, cos_t, sin_t):
    gain = gain.astype(F32)
    return ((gain[:, None] * cos_t).astype(BF16), (_swap_rotary_halves(gain)[:, None] * sin_t).astype(BF16))


def _gain_cols(gain, scale=1.0):
    return jnp.broadcast_to((gain.astype(F32) * scale)[:, None], (HEAD_DIM, QKV_SUB))


def kernel(x, c, ctx, c_ctx, ada_w, ada_b, g_mix, g_mlp, mlp_w1, mlp_w2, na_wqkv, na_q_gain, na_k_gain,
           na_rpb, na_wo, swa_wqkv, swa_q_gain, swa_k_gain, swa_sink, swa_wo):
    b, l, d = x.shape
    n_ctx = ctx.shape[1]
    tm = TM_POST
    tmc = n_ctx
    scale = HEAD_DIM ** -0.5 * LOG2E

    cond =jnp.concatenate([c, c_ctx[None], jnp.zeros((8 - b - 1, d), F32)], axis=0)
    mods = _modulation(cond, ada_w, ada_b)

    def mods_of(i):
        lat = mods[i, :b].reshape(b, 6, d)
        cx = jnp.broadcast_to(mods[i, b].reshape(1, 6, d), (b, 6, d))
        return lat, cx

    def row(v):
        return v.reshape(1, d)

    mod_lat, mod_ctx = mods_of(0)
    wt = na_wqkv[0].T.astype(BF16)
    gq, gk = na_q_gain[0], na_k_gain[0]
    gains = (_gain_cols(gq, scale), _gain_cols(gk))
    qt, k, vt = _qkv_project(x, mod_lat, row(g_mix[0]), wt, gains, None, kv_width=d, tm=TM_QKV)
    qct, kc, vct = _qkv_project(ctx, mod_ctx, row(g_mix[0]), wt, gains, None, kv_width=d, tm=tmc)
    a_lat = _na_attention(qt, k, vt, kc, vct, _na_bias_rows(na_rpb[0] * LOG2E))
    a_ctx = _ctx_attention(qct, kc, vct)
    wo = na_wo[0].astype(BF16)
    w1 = mlp_w1[0].astype(BF16)
    w2 = mlp_w2[0].astype(BF16)
    h_lat = _post_block(a_lat, x, mod_lat, row(g_mlp[0]), wo, w1, w2, tm=tm)
    h_ctx = _post_block(a_ctx, ctx, mod_ctx, row(g_mlp[0]), wo, w1, w2, tm=tmc)

    mod_lat, mod_ctx = mods_of(1)
    kvw = SWA_KV_HEADS * HEAD_DIM
    wt = swa_wqkv[0].T.astype(BF16)
    gq, gk = swa_q_gain[0], swa_k_gain[0]
    cos_t, sin_t = _rope_tables_t(l)
    rope_tabs = (_gain_rope_tables(gq * scale, cos_t, sin_t) + _gain_rope_tables(gk, cos_t, sin_t))
    qt, k, vt = _qkv_project(h_lat, mod_lat, row(g_mix[1]), wt, None, rope_tabs, kv_width=kvw, tm=TM_QKV)
    gains = (_gain_cols(gq, scale), _gain_cols(gk))
    _, kc, vct = _qkv_project(h_ctx, mod_ctx, row(g_mix[1]), wt, gains, None, kv_width=kvw, tm=tmc)
    sink_rows = jnp.repeat(swa_sink[0].astype(F32) * LOG2E, SWA_QB).reshape(SWA_KV_HEADS, 1, SWA_GROUP * SWA_QB)
    a_lat = _swa_attention(qt, k, vt, kc, vct, sink_rows)
    return _post_block(a_lat, h_lat, mod_lat, row(g_mlp[1]), swa_wo[0].astype(BF16),
                       mlp_w1[1].astype(BF16), mlp_w2[1].astype(BF16), tm=tm)
```

```python
import functools

import numpy as np
import jax
import jax.numpy as jnp
from jax import lax
from jax.experimental import pallas as pl
from jax.experimental.pallas import tpu as pltpu

GRID_W = 64
N_HEADS = 16
HEAD_DIM = 64
NA_KH = 8
NA_KW = 16
SWA_KV_HEADS = 4
SWA_GROUP = N_HEADS // SWA_KV_HEADS
SWA_WINDOW = 128
ROPE_BASE = 10000.0
NORM_EPS = 1e-6
NEG_INF = -1e30
LOG2E = float(np.log2(np.e))

F32 = jnp.float32
BF16 = jnp.bfloat16

V7X_VMEM_BYTES = 64 * 1024 * 1024
VMEM_LIMIT_BYTES = V7X_VMEM_BYTES - 8 * 1024 * 1024

BF16_SUBLANES = 16
HEAD_PAIR = 2 * HEAD_DIM
NA_Q_ROWS = 4
NA_K_ROWS = NA_Q_ROWS + NA_KH
NA_QB = NA_Q_ROWS * GRID_W
NA_KB = NA_K_ROWS * GRID_W
SWA_QB = 128
SWA_KB = SWA_QB + 2 * SWA_WINDOW
SWA_UNIT_HEADS = 2
TM_QKV = 1024
TM_POST = 1024
QKV_SUB = 256
QKV_CHUNK = 256
QKV_CHUNK_ROPE = 1024
PIPE_AHEAD = 2
PIPE_SLOTS = 4


def _params(*semantics):
    return pltpu.CompilerParams(dimension_semantics=semantics, vmem_limit_bytes=VMEM_LIMIT_BYTES)


def _const_spec(shape):
    zeros = (0,) * len(shape)
    return pl.BlockSpec(shape, lambda *_: zeros, pipeline_mode=pl.Buffered(1))


def _mod_kernel(cond_ref, w_ref, b_ref, o_ref):
    s = jax.nn.silu(cond_ref[...]).astype(BF16)
    o_ref[0] = jnp.dot(s, w_ref[0].astype(BF16), preferred_element_type=F32) + b_ref[0]


def _modulation(cond, ada_w, ada_b, tn=1536):
    depth, d, n = ada_w.shape
    rows = cond.shape[0]
    return pl.pallas_call(
        _mod_kernel,
        grid=(depth, n // tn),
        in_specs=[
            pl.BlockSpec((rows, d), lambda i, j: (0, 0)),
            pl.BlockSpec((1, d, tn), lambda i, j: (i, 0, j)),
            pl.BlockSpec((1, 1, tn), lambda i, j: (i, 0, j)),
        ],
        out_specs=pl.BlockSpec((1, rows, tn), lambda i, j: (i, 0, j)),
        out_shape=jax.ShapeDtypeStruct((depth, rows, n), F32),
        compiler_params=_params("arbitrary", "arbitrary"),
        name="adaln_modulation",
    )(cond, ada_w, ada_b.reshape(depth, 1, n))


def _rms_modulate(x, g, shift, scale):
    y = x * lax.rsqrt(jnp.mean(x * x, axis=-1, keepdims=True) + NORM_EPS) * g
    return y * (1.0 + scale) + shift


def _head_norm_t(a, gain_b):
    n = a.shape[0] // HEAD_DIM
    tm = a.shape[1]
    a = a.reshape(n, HEAD_DIM, tm)
    a = (a * lax.rsqrt(jnp.mean(a * a, axis=1, keepdims=True) + NORM_EPS)).astype(BF16) * gain_b[None]
    return a.reshape(n * HEAD_DIM, tm)


def _head_norm_rope_t(a, gain_cos, gain_sin):
    n = a.shape[0] // HEAD_DIM
    tm = a.shape[1]
    a = a.reshape(n, HEAD_DIM, tm)
    a = (a * lax.rsqrt(jnp.mean(a * a, axis=1, keepdims=True) + NORM_EPS)).astype(BF16)
    q = HEAD_DIM // 4
    swapped = jnp.concatenate([a[:, q:2 * q], a[:, 0:q], a[:, 3 * q:4 * q], a[:, 2 * q:3 * q]], axis=1)
    a = a * gain_cos[None] + swapped * gain_sin[None]
    return a.reshape(n * HEAD_DIM, tm)


def _qkv_kernel(*refs, d_model, kv_width, use_rope, chunk, sub):
    if use_rope:
        h_ref, mod_ref, g_ref, wt_ref, qc_ref, qs_ref, kc_ref, ks_ref, q_ref, k_ref, v_ref = refs
    else:
        h_ref, mod_ref, g_ref, wt_ref, gq_ref, gk_ref, q_ref, k_ref, v_ref = refs
        gq = gq_ref[...].astype(BF16)
        gk = gk_ref[...].astype(BF16)
    mod = mod_ref[0]
    g_eff = (g_ref[...] * (1.0 + mod[1:2])).astype(BF16)
    shift = mod[0:1].astype(BF16)
    contract_last = (((1,), (1,)), ((), ()))
    kv_chunk = min(chunk, kv_width)
    for t0 in range(0, h_ref.shape[1], sub):
        toks = slice(t0, t0 + sub)
        x = h_ref[0, toks, :]
        xn = (x * lax.rsqrt(jnp.mean(x * x, axis=-1, keepdims=True) + NORM_EPS)).astype(BF16)
        u = xn * g_eff + shift

        def proj(r0, rows):
            return lax.dot_general(wt_ref[r0:r0 + rows, :], u, contract_last,
                                   preferred_element_type=F32)

        def norm_q(a):
            if use_rope:
                return _head_norm_rope_t(a, qc_ref[:, toks], qs_ref[:, toks])
            return _head_norm_t(a, gq)

        def norm_k(a):
            if use_rope:
                return _head_norm_rope_t(a, kc_ref[:, toks], ks_ref[:, toks])
            return _head_norm_t(a, gk)

        for r0 in range(0, d_model, chunk):
            q_ref[0, r0:r0 + chunk, toks] = norm_q(proj(r0, chunk))
        for r0 in range(0, kv_width, kv_chunk):
            k_ref[0, toks, r0:r0 + kv_chunk] = norm_k(proj(d_model + r0, kv_chunk)).T
        for r0 in range(0, kv_width, kv_chunk):
            v_ref[0, r0:r0 + kv_chunk, toks] = proj(d_model + kv_width + r0, kv_chunk).astype(BF16)


def _qkv_project(h, mod, g, wt, gains, rope_tabs, *, kv_width, tm, sub=QKV_SUB):
    b, t, d = h.shape
    use_rope = rope_tabs is not None
    chunk = QKV_CHUNK_ROPE if use_rope else QKV_CHUNK
    in_specs = [
        pl.BlockSpec((1, tm, d), lambda i, j: (i, j, 0)),
        pl.BlockSpec((1, 6, d), lambda i, j: (i, 0, 0)),
        _const_spec((1, d)),
        _const_spec(wt.shape),
    ]
    args = [h, mod, g, wt]
    if use_rope:
        in_specs += [pl.BlockSpec((HEAD_DIM, tm), lambda i, j: (0, j))] * 4
        args += list(rope_tabs)
    else:
        in_specs += [_const_spec((HEAD_DIM, sub))] * 2
        args += list(gains)
    return pl.pallas_call(
        functools.partial(_qkv_kernel, d_model=d, kv_width=kv_width, use_rope=use_rope, chunk=chunk,
                          sub=sub),
        grid=(b, t // tm),
        in_specs=in_specs,
        out_specs=[
            pl.BlockSpec((1, d, tm), lambda i, j: (i, 0, j)),
            pl.BlockSpec((1, tm, kv_width), lambda i, j: (i, j, 0)),
            pl.BlockSpec((1, kv_width, tm), lambda i, j: (i, 0, j)),
        ],
        out_shape=[
            jax.ShapeDtypeStruct((b, d, t), BF16),
            jax.ShapeDtypeStruct((b, t, kv_width), BF16),
            jax.ShapeDtypeStruct((b, kv_width, t), BF16),
        ],
        compiler_params=_params("arbitrary", "arbitrary"),
        name="qkv_rope" if use_rope else "qkv",
    )(*args)


def _col_max(*parts):
    m = parts[0].max(axis=0, keepdims=True)
    for p in parts[1:]:
        m = jnp.maximum(m, p.max(axis=0, keepdims=True))
    return m


def _mask_rows_to_head(q_pair, head):
    row = lax.broadcasted_iota(jnp.int32, q_pair.shape, 0)
    return jnp.where((row // HEAD_DIM) == head, q_pair, jnp.zeros_like(q_pair))


def _na_build_bias(rows_ref, bias_ref):
    w = GRID_W
    kc = lax.broadcasted_iota(jnp.int32, (w, 2 * w), 0)
    qc = lax.broadcasted_iota(jnp.int32, (w, 2 * w), 1) % w
    c0 = jnp.clip(qc - NA_KW // 2, 0, w - NA_KW)
    col_valid = (kc >= c0) & (kc < c0 + NA_KW)
    plan, combos = _na_bias_plan()
    for hh in range(2):
        tiles = []
        for n in range(len(combos)):
            row = jnp.broadcast_to(rows_ref[hh, n:n + 1, :], (w, 2 * w))
            tiles.append(jnp.where(col_valid, pltpu.roll(row, 0, 1, stride=1, stride_axis=0), NEG_INF))
        for v in range(3):
            for kr in range(NA_K_ROWS):
                for half in range(NA_Q_ROWS // 2):
                    bias_ref[hh, v, kr * w:(kr + 1) * w, half * 2 * w:(half + 1) * 2 * w] = tiles[plan[v, kr, half]]


def _na_kernel(q_ref, k_ref, v_ref, kc_ref, vc_ref, rows_ref, o_ref, bias_ref, sl_ref, sc_ref, m_ref, *,
               n_blocks, grid_rows, unroll):
    @pl.when(pl.program_id(1) == 0)
    def _():
        _na_build_bias(rows_ref, bias_ref)

    ones_loc = jnp.ones((BF16_SUBLANES, NA_KB), BF16)
    ones_ctx = jnp.ones((BF16_SUBLANES, kc_ref.shape[1]), BF16)

    def window(i):
        start_row = jnp.clip(NA_Q_ROWS * i - NA_KH // 2, 0, grid_rows - NA_K_ROWS)
        return pl.multiple_of(start_row * GRID_W, NA_QB), pl.multiple_of(i * NA_QB, NA_QB)

    def scores(i, hh, slot):
        i = jnp.minimum(i, n_blocks - 1)
        k0, q0 = window(i)
        variant = jnp.where(i == 0, 0, jnp.where(i == n_blocks - 1, 2, 1))
        qm = _mask_rows_to_head(q_ref[0, :, pl.ds(q0, NA_QB)], hh)
        s_loc = jnp.dot(k_ref[0, pl.ds(k0, NA_KB), :], qm, preferred_element_type=F32) + bias_ref[hh, variant]
        s_ctx = jnp.dot(kc_ref[0], qm, preferred_element_type=F32)
        sl_ref[slot] = s_loc
        sc_ref[slot] = s_ctx
        m_ref[slot] = _col_max(s_loc, s_ctx)

    def attend(i, hh, slot):
        k0, _ = window(i)
        m = m_ref[slot]
        p_loc = jnp.exp2(sl_ref[slot] - m).astype(BF16)
        p_ctx = jnp.exp2(sc_ref[slot] - m).astype(BF16)
        rows = slice(hh * HEAD_DIM, (hh + 1) * HEAD_DIM)
        v_loc = jnp.concatenate([v_ref[0, rows, pl.ds(k0, NA_KB)], ones_loc], axis=0)
        v_ctx = jnp.concatenate([vc_ref[0, rows, :], ones_ctx], axis=0)
        o = (jnp.dot(v_loc, p_loc, preferred_element_type=F32)
             + jnp.dot(v_ctx, p_ctx, preferred_element_type=F32))
        return o[:HEAD_DIM] / o[HEAD_DIM:HEAD_DIM + 1]

    for n in range(PIPE_AHEAD):
        scores(n // 2, n % 2, n)

    single_trip = n_blocks == unroll

    def body(it, carry):
        base = it * unroll
        units = [(base + blk, hh) for blk in range(unroll + (PIPE_AHEAD + 1) // 2) for hh in range(2)]
        pair = []
        for n, (i, hh) in enumerate(units[:2 * unroll]):
            ahead = units[n + PIPE_AHEAD]
            if not (single_trip and ahead[0] >= n_blocks):
                scores(ahead[0], ahead[1], (n + PIPE_AHEAD) % PIPE_SLOTS)
            pair.append(attend(i, hh, n % PIPE_SLOTS))
            if hh == 1:
                q0 = pl.multiple_of(i * NA_QB, NA_QB)
                o_ref[0, pl.ds(q0, NA_QB), :] = jnp.concatenate(pair, axis=0).T.astype(BF16)
                pair = []
        return carry

    if single_trip:
        body(0, 0)
    else:
        lax.fori_loop(0, n_blocks // unroll, body, 0)


def _na_attention(qt, k, vt, kc, vct, bias_rows, unroll=16):
    b, d, l = qt.shape
    c = kc.shape[1]
    pairs = d // HEAD_PAIR
    grid_rows = l // GRID_W
    n_blocks = grid_rows // NA_Q_ROWS
    n_combos = bias_rows.shape[1]
    assert grid_rows >= NA_K_ROWS and n_blocks >= 3 and n_blocks % unroll == 0
    assert (2 * unroll) % PIPE_SLOTS == 0 and PIPE_SLOTS > PIPE_AHEAD
    return pl.pallas_call(
        functools.partial(_na_kernel, n_blocks=n_blocks, grid_rows=grid_rows, unroll=unroll),
        grid=(pairs, b),
        in_specs=[
            pl.BlockSpec((1, HEAD_PAIR, l), lambda p, i: (i, p, 0)),
            pl.BlockSpec((1, l, HEAD_PAIR), lambda p, i: (i, 0, p)),
            pl.BlockSpec((1, HEAD_PAIR, l), lambda p, i: (i, p, 0)),
            pl.BlockSpec((1, c, HEAD_PAIR), lambda p, i: (i, 0, p)),
            pl.BlockSpec((1, HEAD_PAIR, c), lambda p, i: (0, p, i)),
            pl.BlockSpec((2, n_combos, 2 * GRID_W), lambda p, i: (p, 0, 0)),
        ],
        out_specs=pl.BlockSpec((1, l, HEAD_PAIR), lambda p, i: (i, 0, p)),
        out_shape=jax.ShapeDtypeStruct((b, l, d), BF16),
        scratch_shapes=[
            pltpu.VMEM((2, 3, NA_KB, NA_QB), F32),
            pltpu.VMEM((PIPE_SLOTS, NA_KB, NA_QB), F32),
            pltpu.VMEM((PIPE_SLOTS, c, NA_QB), F32),
            pltpu.VMEM((PIPE_SLOTS, 1, NA_QB), F32),
        ],
        compiler_params=_params("arbitrary", "arbitrary"),
        name="neighbourhood_attention",
    )(qt, k, vt, kc, vct, bias_rows)


NA_MASKED_ROW = 2 * NA_KH - 1


@functools.lru_cache(maxsize=None)
def _na_bias_plan():
    kr = np.arange(NA_K_ROWS)[:, None]
    qr = np.arange(NA_Q_ROWS)[None, :]
    top = NA_KH - 1
    variants = [
        (kr < NA_KH + 0 * qr, kr - qr + top),
        ((kr >= qr) & (kr < qr + NA_KH), kr - qr + top - NA_KH // 2),
        (kr >= NA_K_ROWS - NA_KH + 0 * qr, kr - qr + top - (NA_K_ROWS - NA_Q_ROWS)),
    ]
    idx = np.stack([np.where(valid, off, NA_MASKED_ROW) for valid, off in variants])
    assert idx.min() >= 0 and idx.max() <= NA_MASKED_ROW
    combos = []
    plan = np.zeros((3, NA_K_ROWS, NA_Q_ROWS // 2), np.int32)
    for v in range(3):
        for r in range(NA_K_ROWS):
            for half in range(NA_Q_ROWS // 2):
                key = (int(idx[v, r, 2 * half]), int(idx[v, r, 2 * half + 1]))
                if key not in combos:
                    combos.append(key)
                plan[v, r, half] = combos.index(key)
    return plan, tuple(combos)


def _na_bias_rows(rpb):
    h, n_rows, n_off = rpb.shape
    w = GRID_W
    reach = NA_KW - 1
    _, combos = _na_bias_plan()
    masked = n_rows * n_off
    lane = np.arange(2 * w)
    first = np.where(lane <= reach, reach - lane, np.where(lane >= 2 * w - reach, reach + 2 * w - lane, -1))
    second = np.where(np.abs(lane - w) <= reach, reach + w - lane, -1)
    idx = np.full((len(combos), 2 * w), masked, np.int32)
    for n, (a, b) in enumerate(combos):
        if a != NA_MASKED_ROW:
            idx[n] = np.where(first >= 0, a * n_off + first, idx[n])
        if b != NA_MASKED_ROW:
            idx[n] = np.where(second >= 0, b * n_off + second, idx[n])
    flat = jnp.concatenate([rpb.reshape(h, masked), jnp.full((h, 1), NEG_INF, rpb.dtype)], axis=1)
    return flat[:, idx]


def _ctx_kernel(q_ref, k_ref, v_ref, o_ref, *, pairs):
    def scores(p, hh):
        cols = slice(p * HEAD_PAIR, (p + 1) * HEAD_PAIR)
        return jnp.dot(k_ref[0, :, cols], _mask_rows_to_head(q_ref[0, cols, :], hh),
                       preferred_element_type=F32)

    heads = [(p, hh) for p in range(pairs) for hh in range(2)]
    s_next = scores(*heads[0])
    outs = []
    for n, (p, hh) in enumerate(heads):
        s = s_next
        if n + 1 < len(heads):
            s_next = scores(*heads[n + 1])
        pr = jnp.exp2(s - s.max(axis=0, keepdims=True))
        l = pr.sum(axis=0, keepdims=True)
        rows = slice(p * HEAD_PAIR + hh * HEAD_DIM, p * HEAD_PAIR + (hh + 1) * HEAD_DIM)
        outs.append(jnp.dot(v_ref[0, rows, :], pr.astype(BF16), preferred_element_type=F32) / l)
        if hh == 1:
            cols = slice(p * HEAD_PAIR, (p + 1) * HEAD_PAIR)
            o_ref[0, :, cols] = jnp.concatenate(outs, axis=0).T.astype(BF16)
            outs = []


def _ctx_attention(qct, kc, vct):
    b, c, d = kc.shape
    return pl.pallas_call(
        functools.partial(_ctx_kernel, pairs=d // HEAD_PAIR),
        grid=(b,),
        in_specs=[
            pl.BlockSpec((1, d, c), lambda i: (0, 0, i)),
            pl.BlockSpec((1, c, d), lambda i: (i, 0, 0)),
            pl.BlockSpec((1, d, c), lambda i: (0, 0, i)),
        ],
        out_specs=pl.BlockSpec((1, c, d), lambda i: (i, 0, 0)),
        out_shape=jax.ShapeDtypeStruct((b, c, d), BF16),
        compiler_params=_params("arbitrary"),
        name="context_attention",
    )(qct, kc, vct)


def _swa_kernel(q_ref, k_ref, v_ref, kc_ref, vc_ref, mask_ref, sink_ref, o_ref, sl_ref, sc_ref, m_ref, *,
                n_blocks, seq, unroll):
    kv_in_pair = pl.program_id(1) % 2
    ones_loc = jnp.ones((BF16_SUBLANES, SWA_KB), BF16)
    ones_ctx = jnp.ones((BF16_SUBLANES, kc_ref.shape[1]), BF16)

    def window(j):
        k0 = pl.multiple_of(jnp.clip(SWA_QB * j - SWA_WINDOW, 0, seq - SWA_KB), SWA_QB)
        return k0, pl.multiple_of(j * SWA_QB, SWA_QB)

    hp = SWA_UNIT_HEADS
    parts = SWA_GROUP // hp

    def scores(j, part, slot):
        j = jnp.minimum(j, n_blocks - 1)
        k0, q0 = window(j)
        variant = jnp.where(j == 0, 0, jnp.where(j == n_blocks - 1, 2, 1))
        heads = range(part * hp, (part + 1) * hp)
        qg = jnp.concatenate(
            [q_ref[0, g * HEAD_DIM:(g + 1) * HEAD_DIM, pl.ds(q0, SWA_QB)] for g in heads], axis=1)
        qm = _mask_rows_to_head(jnp.concatenate([qg, qg], axis=0), kv_in_pair)
        mask = mask_ref[variant]
        s_loc = (jnp.dot(k_ref[0, pl.ds(k0, SWA_KB), :], qm, preferred_element_type=F32)
                 + jnp.concatenate([mask] * hp, axis=1))
        s_ctx = jnp.dot(kc_ref[0], qm, preferred_element_type=F32)
        sl_ref[slot] = s_loc
        sc_ref[slot] = s_ctx
        m_ref[slot] = jnp.maximum(_col_max(s_loc, s_ctx), sink_ref[0, :, part * hp * SWA_QB:(part + 1) * hp * SWA_QB])

    def attend(j, part, slot):
        k0, q0 = window(j)
        m = m_ref[slot]
        sink = sink_ref[0, :, part * hp * SWA_QB:(part + 1) * hp * SWA_QB]
        p_loc = jnp.exp2(sl_ref[slot] - m).astype(BF16)
        p_ctx = jnp.exp2(sc_ref[slot] - m).astype(BF16)
        v_loc = jnp.concatenate([v_ref[0, :, pl.ds(k0, SWA_KB)], ones_loc], axis=0)
        v_ctx = jnp.concatenate([vc_ref[0], ones_ctx], axis=0)
        o = (jnp.dot(v_loc, p_loc, preferred_element_type=F32)
             + jnp.dot(v_ctx, p_ctx, preferred_element_type=F32))
        o = o[:HEAD_DIM] / (o[HEAD_DIM:HEAD_DIM + 1] + jnp.exp2(sink - m))
        o_rows = jnp.concatenate([o[:, g * SWA_QB:(g + 1) * SWA_QB] for g in range(hp)], axis=0)
        o_ref[0, pl.ds(q0, SWA_QB), part * hp * HEAD_DIM:(part + 1) * hp * HEAD_DIM] = o_rows.T.astype(BF16)

    for n in range(PIPE_AHEAD):
        scores(n // parts, n % parts, n)

    def body(it, carry):
        base = it * unroll
        for n in range(unroll * parts):
            ahead = n + PIPE_AHEAD
            scores(base + ahead // parts, ahead % parts, ahead % PIPE_SLOTS)
            attend(base + n // parts, n % parts, n % PIPE_SLOTS)
        return carry

    lax.fori_loop(0, n_blocks // unroll, body, 0)


def _swa_mask_tables():
    kr = np.arange(SWA_KB)[:, None]
    qc = np.arange(SWA_QB)[None, :]
    rel = [kr - qc, kr - SWA_WINDOW - qc, kr - (SWA_KB - SWA_QB) - qc]
    return np.stack([np.where(np.abs(r) <= SWA_WINDOW, 0.0, NEG_INF) for r in rel]).astype(np.float32)


def _swa_attention(qt, k, vt, kc, vct, sink_rows, unroll=16):
    b, d, l = qt.shape
    c = kc.shape[1]
    kv = k.shape[2] // HEAD_DIM
    gw = SWA_GROUP * HEAD_DIM
    n_blocks = l // SWA_QB
    assert l >= SWA_KB and n_blocks >= 3 and n_blocks % unroll == 0
    assert (unroll * SWA_GROUP // SWA_UNIT_HEADS) % PIPE_SLOTS == 0 and PIPE_SLOTS > PIPE_AHEAD
    return pl.pallas_call(
        functools.partial(_swa_kernel, n_blocks=n_blocks, seq=l, unroll=unroll),
        grid=(b, kv),
        in_specs=[
            pl.BlockSpec((1, gw, l), lambda i, n: (i, n, 0)),
            pl.BlockSpec((1, l, HEAD_PAIR), lambda i, n: (i, 0, n // 2)),
            pl.BlockSpec((1, HEAD_DIM, l), lambda i, n: (i, n, 0)),
            pl.BlockSpec((1, c, HEAD_PAIR), lambda i, n: (i, 0, n // 2)),
            pl.BlockSpec((1, HEAD_DIM, c), lambda i, n: (0, n, i)),
            _const_spec((3, SWA_KB, SWA_QB)),
            pl.BlockSpec((1, 1, SWA_GROUP * SWA_QB), lambda i, n: (n, 0, 0)),
        ],
        out_specs=pl.BlockSpec((1, l, gw), lambda i, n: (i, 0, n)),
        out_shape=jax.ShapeDtypeStruct((b, l, d), BF16),
        scratch_shapes=[
            pltpu.VMEM((PIPE_SLOTS, SWA_KB, SWA_UNIT_HEADS * SWA_QB), F32),
            pltpu.VMEM((PIPE_SLOTS, c, SWA_UNIT_HEADS * SWA_QB), F32),
            pltpu.VMEM((PIPE_SLOTS, 1, SWA_UNIT_HEADS * SWA_QB), F32),
        ],
        compiler_params=_params("arbitrary", "arbitrary"),
        name="sliding_window_attention",
    )(qt, k, vt, kc, vct, jnp.asarray(_swa_mask_tables()), sink_rows)


def _post_kernel(a_ref, h_ref, mod_ref, g_ref, wo_ref, w1_ref, w2_ref, o_ref, *, ff_chunk):
    mod = mod_ref[0]
    y = jnp.dot(a_ref[0], wo_ref[...], preferred_element_type=F32)
    h1 = h_ref[0] + mod[2:3] * y
    u = _rms_modulate(h1, g_ref[...], mod[3:4], mod[4:5]).astype(BF16)
    d_ff = w1_ref.shape[1]
    acc = jnp.zeros_like(h1)
    for c0 in range(0, d_ff, ff_chunk):
        t = jnp.dot(u, w1_ref[:, c0:c0 + ff_chunk], preferred_element_type=F32)
        r = jnp.square(jnp.maximum(t, 0.0)).astype(BF16)
        acc = acc + jnp.dot(r, w2_ref[c0:c0 + ff_chunk, :], preferred_element_type=F32)
    o_ref[0] = h1 + mod[5:6] * acc


def _post_block(a, h, mod, g, wo, w1, w2, *, tm, ff_chunk=1024):
    b, t, d = h.shape
    return pl.pallas_call(
        functools.partial(_post_kernel, ff_chunk=ff_chunk),
        grid=(b, t // tm),
        in_specs=[
            pl.BlockSpec((1, tm, d), lambda i, j: (i, j, 0)),
            pl.BlockSpec((1, tm, d), lambda i, j: (i, j, 0)),
            pl.BlockSpec((1, 6, d), lambda i, j: (i, 0, 0)),
            _const_spec((1, d)),
            _const_spec(wo.shape),
            _const_spec(w1.shape),
            _const_spec(w2.shape),
        ],
        out_specs=pl.BlockSpec((1, tm, d), lambda i, j: (i, j, 0)),
        out_shape=jax.ShapeDtypeStruct((b, t, d), F32),
        compiler_params=_params("arbitrary", "arbitrary"),
        name="outproj_mlp",
    )(a, h, mod, g, wo, w1, w2)


def _rope_tables_t(l):
    t = np.arange(l)
    n_freq = HEAD_DIM // 4
    inv = ROPE_BASE ** (-np.arange(n_freq, dtype=np.float64) / n_freq)
    ang = np.stack([(t // GRID_W)[:, None] * inv, (t % GRID_W)[:, None] * inv], axis=1)
    cos = np.transpose(np.cos(ang), (1, 2, 0))
    sin = np.transpose(np.sin(ang), (1, 2, 0))
    cos_t = np.stack([cos, cos], axis=1).reshape(HEAD_DIM, l)
    sin_t = np.stack([-sin, sin], axis=1).reshape(HEAD_DIM, l)
    return jnp.asarray(cos_t, F32), jnp.asarray(sin_t, F32)


def _swap_rotary_halves(v):
    q = HEAD_DIM // 4
    return jnp.concatenate([v[q:2 * q], v[0:q], v[3 * q:4 * q], v[2 * q:3 * q]], axis=0)


def _gain_rope_tables(gain, cos_t, sin_t):
    gain = gain.astype(F32)
    return ((gain[:, None] * cos_t).astype(BF16), (_swap_rotary_halves(gain)[:, None] * sin_t).astype(BF16))


def _gain_cols(gain, scale=1.0):
    return jnp.broadcast_to((gain.astype(F32) * scale)[:, None], (HEAD_DIM, QKV_SUB))


def kernel(x, c, ctx, c_ctx, ada_w, ada_b, g_mix, g_mlp, mlp_w1, mlp_w2, na_wqkv, na_q_gain, na_k_gain,
           na_rpb, na_wo, swa_wqkv, swa_q_gain, swa_k_gain, swa_sink, swa_wo):
    b, l, d = x.shape
    n_ctx = ctx.shape[1]
    tm = TM_POST
    tmc = b * n_ctx
    ctx_flat = ctx.reshape(1, tmc, d)
    scale = HEAD_DIM ** -0.5 * LOG2E

    cond =jnp.concatenate([c, c_ctx[None], jnp.zeros((8 - b - 1, d), F32)], axis=0)
    mods = _modulation(cond, ada_w, ada_b)

    def mods_of(i):
        lat = mods[i, :b].reshape(b, 6, d)
        cx = mods[i, b].reshape(1, 6, d)
        return lat, cx

    def row(v):
        return v.reshape(1, d)

    mod_lat, mod_ctx = mods_of(0)
    wt = na_wqkv[0].T.astype(BF16)
    gq, gk = na_q_gain[0], na_k_gain[0]
    gains = (_gain_cols(gq, scale), _gain_cols(gk))
    qt, k, vt = _qkv_project(x, mod_lat, row(g_mix[0]), wt, gains, None, kv_width=d, tm=TM_QKV)
    qct, kc, vct = _qkv_project(ctx_flat, mod_ctx, row(g_mix[0]), wt, gains, None, kv_width=d, tm=tmc)
    kc = kc.reshape(b, n_ctx, d)
    a_lat = _na_attention(qt, k, vt, kc, vct, _na_bias_rows(na_rpb[0] * LOG2E))
    a_ctx = _ctx_attention(qct, kc, vct)
    wo = na_wo[0].astype(BF16)
    w1 = mlp_w1[0].astype(BF16)
    w2 = mlp_w2[0].astype(BF16)
    h_lat = _post_block(a_lat, x, mod_lat, row(g_mlp[0]), wo, w1, w2, tm=tm)
    h_ctx = _post_block(a_ctx.reshape(1, tmc, d), ctx_flat, mod_ctx, row(g_mlp[0]), wo, w1, w2, tm=tmc)

    mod_lat, mod_ctx = mods_of(1)
    kvw = SWA_KV_HEADS * HEAD_DIM
    wt = swa_wqkv[0].T.astype(BF16)
    gq, gk = swa_q_gain[0], swa_k_gain[0]
    cos_t, sin_t = _rope_tables_t(l)
    rope_tabs = (_gain_rope_tables(gq * scale, cos_t, sin_t) + _gain_rope_tables(gk, cos_t, sin_t))
    qt, k, vt = _qkv_project(h_lat, mod_lat, row(g_mix[1]), wt, None, rope_tabs, kv_width=kvw, tm=TM_QKV)
    gains = (_gain_cols(gq, scale), _gain_cols(gk))
    _, kc, vct = _qkv_project(h_ctx, mod_ctx, row(g_mix[1]), wt, gains, None, kv_width=kvw, tm=tmc)
    kc = kc.reshape(b, n_ctx, kvw)
    sink_rows = jnp.repeat(swa_sink[0].astype(F32) * LOG2E, SWA_QB).reshape(SWA_KV_HEADS, 1, SWA_GROUP * SWA_QB)
    a_lat = _swa_attention(qt, k, vt, kc, vct, sink_rows)
    return _post_block(a_lat, h_lat, mod_lat, row(g_mlp[1]), swa_wo[0].astype(BF16),
                       mlp_w1[1].astype(BF16), mlp_w2[1].astype(BF16), tm=tm)
```

```python
import functools

import numpy as np
import jax
import jax.numpy as jnp
from jax import lax
from jax.experimental import pallas as pl
from jax.experimental.pallas import tpu as pltpu

GRID_W = 64
N_HEADS = 16
HEAD_DIM = 64
NA_KH = 8
NA_KW = 16
SWA_KV_HEADS = 4
SWA_GROUP = N_HEADS // SWA_KV_HEADS
SWA_WINDOW = 128
ROPE_BASE = 10000.0
NORM_EPS = 1e-6
NEG_INF = -1e30
LOG2E = float(np.log2(np.e))

F32 = jnp.float32
BF16 = jnp.bfloat16

V7X_VMEM_BYTES = 64 * 1024 * 1024
VMEM_LIMIT_BYTES = V7X_VMEM_BYTES - 8 * 1024 * 1024

BF16_SUBLANES = 16
HEAD_PAIR = 2 * HEAD_DIM
NA_Q_ROWS = 4
NA_K_ROWS = NA_Q_ROWS + NA_KH
NA_QB = NA_Q_ROWS * GRID_W
NA_KB = NA_K_ROWS * GRID_W
SWA_QB = 128
SWA_KB = SWA_QB + 2 * SWA_WINDOW
SWA_UNIT_HEADS = 2
TM_QKV = 1024
TM_POST = 1024
QKV_SUB = 256
QKV_CHUNK = 256
QKV_CHUNK_ROPE = 1024
PIPE_AHEAD = 2
PIPE_SLOTS = 4


def _params(*semantics):
    return pltpu.CompilerParams(dimension_semantics=semantics, vmem_limit_bytes=VMEM_LIMIT_BYTES)


def _const_spec(shape):
    zeros = (0,) * len(shape)
    return pl.BlockSpec(shape, lambda *_: zeros, pipeline_mode=pl.Buffered(1))


def _mod_kernel(cond_ref, w_ref, b_ref, o_ref):
    s = jax.nn.silu(cond_ref[...]).astype(BF16)
    o_ref[0] = jnp.dot(s, w_ref[0].astype(BF16), preferred_element_type=F32) + b_ref[0]


def _modulation(cond, ada_w, ada_b, tn=1536):
    depth, d, n = ada_w.shape
    rows = cond.shape[0]
    return pl.pallas_call(
        _mod_kernel,
        grid=(depth, n // tn),
        in_specs=[
            pl.BlockSpec((rows, d), lambda i, j: (0, 0)),
            pl.BlockSpec((1, d, tn), lambda i, j: (i, 0, j)),
            pl.BlockSpec((1, 1, tn), lambda i, j: (i, 0, j)),
        ],
        out_specs=pl.BlockSpec((1, rows, tn), lambda i, j: (i, 0, j)),
        out_shape=jax.ShapeDtypeStruct((depth, rows, n), F32),
        compiler_params=_params("arbitrary", "arbitrary"),
        name="adaln_modulation",
    )(cond, ada_w, ada_b.reshape(depth, 1, n))


def _rms_modulate(x, g, shift, scale):
    y = x * lax.rsqrt(jnp.mean(x * x, axis=-1, keepdims=True) + NORM_EPS) * g
    return y * (1.0 + scale) + shift


def _head_norm_t(a, gain_b):
    n = a.shape[0] // HEAD_DIM
    tm = a.shape[1]
    a = a.reshape(n, HEAD_DIM, tm)
    a = (a * lax.rsqrt(jnp.mean(a * a, axis=1, keepdims=True) + NORM_EPS)).astype(BF16) * gain_b[None]
    return a.reshape(n * HEAD_DIM, tm)


def _head_norm_rope_t(a, gain_cos, gain_sin):
    n = a.shape[0] // HEAD_DIM
    tm = a.shape[1]
    a = a.reshape(n, HEAD_DIM, tm)
    a = (a * lax.rsqrt(jnp.mean(a * a, axis=1, keepdims=True) + NORM_EPS)).astype(BF16)
    q = HEAD_DIM // 4
    swapped = jnp.concatenate([a[:, q:2 * q], a[:, 0:q], a[:, 3 * q:4 * q], a[:, 2 * q:3 * q]], axis=1)
    a = a * gain_cos[None] + swapped * gain_sin[None]
    return a.reshape(n * HEAD_DIM, tm)


def _qkv_kernel(*refs, d_model, kv_width, use_rope, chunk, sub, prepare_weight):
    if prepare_weight:
        *refs, wt_ref = refs
    if use_rope:
        h_ref, mod_ref, g_ref, w_ref, qc_ref, qs_ref, kc_ref, ks_ref, q_ref, k_ref, v_ref = refs
    else:
        h_ref, mod_ref, g_ref, w_ref, gq_ref, gk_ref, q_ref, k_ref, v_ref = refs
        gq = gq_ref[...].astype(BF16)
        gk = gk_ref[...].astype(BF16)
    if prepare_weight:
        @pl.when((pl.program_id(0) == 0) & (pl.program_id(1) == 0))
        def _():
            for c0 in range(0, wt_ref.shape[0], QKV_SUB):
                wt_ref[c0:c0 + QKV_SUB, :] = w_ref[:, c0:c0 + QKV_SUB].T.astype(BF16)
    else:
        wt_ref = w_ref
    mod = mod_ref[0]
    g_eff = (g_ref[...] * (1.0 + mod[1:2])).astype(BF16)
    shift = mod[0:1].astype(BF16)
    contract_last = (((1,), (1,)), ((), ()))
    kv_chunk = min(chunk, kv_width)
    for t0 in range(0, h_ref.shape[1], sub):
        toks = slice(t0, t0 + sub)
        x = h_ref[0, toks, :]
        xn = (x * lax.rsqrt(jnp.mean(x * x, axis=-1, keepdims=True) + NORM_EPS)).astype(BF16)
        u = xn * g_eff + shift

        def proj(r0, rows):
            return lax.dot_general(wt_ref[r0:r0 + rows, :], u, contract_last,
                                   preferred_element_type=F32)

        def norm_q(a):
            if use_rope:
                return _head_norm_rope_t(a, qc_ref[:, toks], qs_ref[:, toks])
            return _head_norm_t(a, gq)

        def norm_k(a):
            if use_rope:
                return _head_norm_rope_t(a, kc_ref[:, toks], ks_ref[:, toks])
            return _head_norm_t(a, gk)

        for r0 in range(0, d_model, chunk):
            q_ref[0, r0:r0 + chunk, toks] = norm_q(proj(r0, chunk))
        for r0 in range(0, kv_width, kv_chunk):
            kt = norm_k(proj(d_model + r0, kv_chunk)).T
            for c0 in range(0, kv_chunk, HEAD_PAIR):
                k_ref[0, (r0 + c0) // HEAD_PAIR, toks, :] = kt[:, c0:c0 + HEAD_PAIR]
        for r0 in range(0, kv_width, kv_chunk):
            v_ref[0, r0:r0 + kv_chunk, toks] = proj(d_model + kv_width + r0, kv_chunk).astype(BF16)


def _qkv_project(h, mod, g, w, gains, rope_tabs, *, kv_width, tm, sub=QKV_SUB):
    b, t, d = h.shape
    use_rope = rope_tabs is not None
    prepare_weight = w.dtype == F32
    n_out = d + 2 * kv_width
    assert w.shape == ((d, n_out) if prepare_weight else (n_out, d))
    chunk = QKV_CHUNK_ROPE if use_rope else QKV_CHUNK
    in_specs = [
        pl.BlockSpec((1, tm, d), lambda i, j: (i, j, 0)),
        pl.BlockSpec((1, 6, d), lambda i, j: (i, 0, 0)),
        _const_spec((1, d)),
        _const_spec(w.shape),
    ]
    args = [h, mod, g, w]
    if use_rope:
        in_specs += [pl.BlockSpec((HEAD_DIM, tm), lambda i, j: (0, j))] * 4
        args += list(rope_tabs)
    else:
        in_specs += [_const_spec((HEAD_DIM, sub))] * 2
        args += list(gains)
    out_specs = [
        pl.BlockSpec((1, d, tm), lambda i, j: (i, 0, j)),
        pl.BlockSpec((1, kv_width // HEAD_PAIR, tm, HEAD_PAIR), lambda i, j: (i, 0, j, 0)),
        pl.BlockSpec((1, kv_width, tm), lambda i, j: (i, 0, j)),
    ]
    out_shape = [
        jax.ShapeDtypeStruct((b, d, t), BF16),
        jax.ShapeDtypeStruct((b, kv_width // HEAD_PAIR, t, HEAD_PAIR), BF16),
        jax.ShapeDtypeStruct((b, kv_width, t), BF16),
    ]
    if prepare_weight:
        out_specs.append(_const_spec((n_out, d)))
        out_shape.append(jax.ShapeDtypeStruct((n_out, d), BF16))
    return pl.pallas_call(
        functools.partial(_qkv_kernel, d_model=d, kv_width=kv_width, use_rope=use_rope, chunk=chunk,
                          sub=sub, prepare_weight=prepare_weight),
        grid=(b, t // tm),
        in_specs=in_specs,
        out_specs=out_specs,
        out_shape=out_shape,
        compiler_params=_params("arbitrary", "arbitrary"),
        name="qkv_rope" if use_rope else "qkv",
    )(*args)


def _col_max(*parts):
    m = parts[0].max(axis=0, keepdims=True)
    for p in parts[1:]:
        m = jnp.maximum(m, p.max(axis=0, keepdims=True))
    return m


def _mask_rows_to_head(q_pair, head):
    row = lax.broadcasted_iota(jnp.int32, q_pair.shape, 0)
    return jnp.where((row // HEAD_DIM) == head, q_pair, jnp.zeros_like(q_pair))


def _na_build_bias(rows_ref, bias_ref):
    w = GRID_W
    kc = lax.broadcasted_iota(jnp.int32, (w, 2 * w), 0)
    qc = lax.broadcasted_iota(jnp.int32, (w, 2 * w), 1) % w
    c0 = jnp.clip(qc - NA_KW // 2, 0, w - NA_KW)
    col_valid = (kc >= c0) & (kc < c0 + NA_KW)
    plan, combos = _na_bias_plan()
    for hh in range(2):
        tiles = []
        for n in range(len(combos)):
            row = jnp.broadcast_to(rows_ref[hh, n:n + 1, :], (w, 2 * w))
            tiles.append(jnp.where(col_valid, pltpu.roll(row, 0, 1, stride=1, stride_axis=0), NEG_INF))
        for v in range(3):
            for kr in range(NA_K_ROWS):
                for half in range(NA_Q_ROWS // 2):
                    bias_ref[hh, v, kr * w:(kr + 1) * w, half * 2 * w:(half + 1) * 2 * w] = tiles[plan[v, kr, half]]


def _na_kernel(q_ref, k_ref, v_ref, kc_ref, vc_ref, rows_ref, o_ref, bias_ref, sl_ref, sc_ref, m_ref, *,
               n_blocks, grid_rows, unroll):
    @pl.when(pl.program_id(1) == 0)
    def _():
        _na_build_bias(rows_ref, bias_ref)

    ones_loc = jnp.ones((BF16_SUBLANES, NA_KB), BF16)
    ones_ctx = jnp.ones((BF16_SUBLANES, kc_ref.shape[2]), BF16)

    def window(i):
        start_row = jnp.clip(NA_Q_ROWS * i - NA_KH // 2, 0, grid_rows - NA_K_ROWS)
        return pl.multiple_of(start_row * GRID_W, NA_QB), pl.multiple_of(i * NA_QB, NA_QB)

    def scores(i, hh, slot):
        i = jnp.minimum(i, n_blocks - 1)
        k0, q0 = window(i)
        variant = jnp.where(i == 0, 0, jnp.where(i == n_blocks - 1, 2, 1))
        qm = _mask_rows_to_head(q_ref[0, :, pl.ds(q0, NA_QB)], hh)
        s_loc = (jnp.dot(k_ref[0, 0, pl.ds(k0, NA_KB), :], qm, preferred_element_type=F32)
                 + bias_ref[hh, variant])
        s_ctx = jnp.dot(kc_ref[0, 0], qm, preferred_element_type=F32)
        sl_ref[slot] = s_loc
        sc_ref[slot] = s_ctx
        m_ref[slot] = _col_max(s_loc, s_ctx)

    def attend(i, hh, slot):
        k0, _ = window(i)
        m = m_ref[slot]
        p_loc = jnp.exp2(sl_ref[slot] - m).astype(BF16)
        p_ctx = jnp.exp2(sc_ref[slot] - m).astype(BF16)
        rows = slice(hh * HEAD_DIM, (hh + 1) * HEAD_DIM)
        v_loc = jnp.concatenate([v_ref[0, rows, pl.ds(k0, NA_KB)], ones_loc], axis=0)
        v_ctx = jnp.concatenate([vc_ref[0, rows, :], ones_ctx], axis=0)
        o = (jnp.dot(v_loc, p_loc, preferred_element_type=F32)
             + jnp.dot(v_ctx, p_ctx, preferred_element_type=F32))
        return o[:HEAD_DIM] / o[HEAD_DIM:HEAD_DIM + 1]

    for n in range(PIPE_AHEAD):
        scores(n // 2, n % 2, n)

    single_trip = n_blocks == unroll

    def body(it, carry):
        base = it * unroll
        units = [(base + blk, hh) for blk in range(unroll + (PIPE_AHEAD + 1) // 2) for hh in range(2)]
        pair = []
        for n, (i, hh) in enumerate(units[:2 * unroll]):
            ahead = units[n + PIPE_AHEAD]
            if not (single_trip and ahead[0] >= n_blocks):
                scores(ahead[0], ahead[1], (n + PIPE_AHEAD) % PIPE_SLOTS)
            pair.append(attend(i, hh, n % PIPE_SLOTS))
            if hh == 1:
                q0 = pl.multiple_of(i * NA_QB, NA_QB)
                o_ref[0, 0, pl.ds(q0, NA_QB), :] = jnp.concatenate(pair, axis=0).T.astype(BF16)
                pair = []
        return carry

    if single_trip:
        body(0, 0)
    else:
        lax.fori_loop(0, n_blocks // unroll, body, 0)


def _na_attention(qt, k, vt, kc, vct, bias_rows, unroll=16):
    b, d, l = qt.shape
    c = vct.shape[2] // b
    pairs = d // HEAD_PAIR
    grid_rows = l // GRID_W
    n_blocks = grid_rows // NA_Q_ROWS
    n_combos = bias_rows.shape[1]
    assert grid_rows >= NA_K_ROWS and n_blocks >= 3 and n_blocks % unroll == 0
    assert (2 * unroll) % PIPE_SLOTS == 0 and PIPE_SLOTS > PIPE_AHEAD
    return pl.pallas_call(
        functools.partial(_na_kernel, n_blocks=n_blocks, grid_rows=grid_rows, unroll=unroll),
        grid=(pairs, b),
        in_specs=[
            pl.BlockSpec((1, HEAD_PAIR, l), lambda p, i: (i, p, 0)),
            pl.BlockSpec((1, 1, l, HEAD_PAIR), lambda p, i: (i, p, 0, 0)),
            pl.BlockSpec((1, HEAD_PAIR, l), lambda p, i: (i, p, 0)),
            pl.BlockSpec((1, 1, c, HEAD_PAIR), lambda p, i: (0, p, i, 0)),
            pl.BlockSpec((1, HEAD_PAIR, c), lambda p, i: (0, p, i)),
            pl.BlockSpec((2, n_combos, 2 * GRID_W), lambda p, i: (p, 0, 0)),
        ],
        out_specs=pl.BlockSpec((1, 1, l, HEAD_PAIR), lambda p, i: (i, p, 0, 0)),
        out_shape=jax.ShapeDtypeStruct((b, pairs, l, HEAD_PAIR), BF16),
        scratch_shapes=[
            pltpu.VMEM((2, 3, NA_KB, NA_QB), F32),
            pltpu.VMEM((PIPE_SLOTS, NA_KB, NA_QB), F32),
            pltpu.VMEM((PIPE_SLOTS, c, NA_QB), F32),
            pltpu.VMEM((PIPE_SLOTS, 1, NA_QB), F32),
        ],
        compiler_params=_params("arbitrary", "arbitrary"),
        name="neighbourhood_attention",
    )(qt, k, vt, kc, vct, bias_rows)


NA_MASKED_ROW = 2 * NA_KH - 1


@functools.lru_cache(maxsize=None)
def _na_bias_plan():
    kr = np.arange(NA_K_ROWS)[:, None]
    qr = np.arange(NA_Q_ROWS)[None, :]
    top = NA_KH - 1
    variants = [
        (kr < NA_KH + 0 * qr, kr - qr + top),
        ((kr >= qr) & (kr < qr + NA_KH), kr - qr + top - NA_KH // 2),
        (kr >= NA_K_ROWS - NA_KH + 0 * qr, kr - qr + top - (NA_K_ROWS - NA_Q_ROWS)),
    ]
    idx = np.stack([np.where(valid, off, NA_MASKED_ROW) for valid, off in variants])
    assert idx.min() >= 0 and idx.max() <= NA_MASKED_ROW
    combos = []
    plan = np.zeros((3, NA_K_ROWS, NA_Q_ROWS // 2), np.int32)
    for v in range(3):
        for r in range(NA_K_ROWS):
            for half in range(NA_Q_ROWS // 2):
                key = (int(idx[v, r, 2 * half]), int(idx[v, r, 2 * half + 1]))
                if key not in combos:
                    combos.append(key)
                plan[v, r, half] = combos.index(key)
    return plan, tuple(combos)


def _na_bias_rows(rpb):
    h, n_rows, n_off = rpb.shape
    w = GRID_W
    reach = NA_KW - 1
    _, combos = _na_bias_plan()
    masked = n_rows * n_off
    lane = np.arange(2 * w)
    first = np.where(lane <= reach, reach - lane, np.where(lane >= 2 * w - reach, reach + 2 * w - lane, -1))
    second = np.where(np.abs(lane - w) <= reach, reach + w - lane, -1)
    idx = np.full((len(combos), 2 * w), masked, np.int32)
    for n, (a, b) in enumerate(combos):
        if a != NA_MASKED_ROW:
            idx[n] = np.where(first >= 0, a * n_off + first, idx[n])
        if b != NA_MASKED_ROW:
            idx[n] = np.where(second >= 0, b * n_off + second, idx[n])
    flat = jnp.concatenate([rpb.reshape(h, masked), jnp.full((h, 1), NEG_INF, rpb.dtype)], axis=1)
    return flat[:, idx]


def _ctx_kernel(q_ref, k_ref, v_ref, o_ref, *, pairs):
    def scores(p, hh):
        cols = slice(p * HEAD_PAIR, (p + 1) * HEAD_PAIR)
        return jnp.dot(k_ref[0, p], _mask_rows_to_head(q_ref[0, cols, :], hh),
                       preferred_element_type=F32)

    heads = [(p, hh) for p in range(pairs) for hh in range(2)]
    s_next = scores(*heads[0])
    outs = []
    for n, (p, hh) in enumerate(heads):
        s = s_next
        if n + 1 < len(heads):
            s_next = scores(*heads[n + 1])
        pr = jnp.exp2(s - s.max(axis=0, keepdims=True))
        l = pr.sum(axis=0, keepdims=True)
        rows = slice(p * HEAD_PAIR + hh * HEAD_DIM, p * HEAD_PAIR + (hh + 1) * HEAD_DIM)
        outs.append(jnp.dot(v_ref[0, rows, :], pr.astype(BF16), preferred_element_type=F32) / l)
        if hh == 1:
            o_ref[0, p] = jnp.concatenate(outs, axis=0).T.astype(BF16)
            outs = []


def _ctx_attention(qct, kc, vct, batches):
    _, d, bc = qct.shape
    c = bc // batches
    pairs = d // HEAD_PAIR
    return pl.pallas_call(
        functools.partial(_ctx_kernel, pairs=pairs),
        grid=(batches,),
        in_specs=[
            pl.BlockSpec((1, d, c), lambda i: (0, 0, i)),
            pl.BlockSpec((1, pairs, c, HEAD_PAIR), lambda i: (0, 0, i, 0)),
            pl.BlockSpec((1, d, c), lambda i: (0, 0, i)),
        ],
        out_specs=pl.BlockSpec((1, pairs, c, HEAD_PAIR), lambda i: (0, 0, i, 0)),
        out_shape=jax.ShapeDtypeStruct((1, pairs, bc, HEAD_PAIR), BF16),
        compiler_params=_params("arbitrary"),
        name="context_attention",
    )(qct, kc, vct)


def _swa_kernel(q_ref, k_ref, v_ref, kc_ref, vc_ref, mask_ref, sink_ref, o_ref, sl_ref, sc_ref, m_ref, *,
                n_blocks, seq, unroll):
    kv_in_pair = pl.program_id(1) % 2
    ones_loc = jnp.ones((BF16_SUBLANES, SWA_KB), BF16)
    ones_ctx = jnp.ones((BF16_SUBLANES, kc_ref.shape[2]), BF16)

    def window(j):
        k0 = pl.multiple_of(jnp.clip(SWA_QB * j - SWA_WINDOW, 0, seq - SWA_KB), SWA_QB)
        return k0, pl.multiple_of(j * SWA_QB, SWA_QB)

    hp = SWA_UNIT_HEADS
    parts = SWA_GROUP // hp

    def scores(j, part, slot):
        j = jnp.minimum(j, n_blocks - 1)
        k0, q0 = window(j)
        variant = jnp.where(j == 0, 0, jnp.where(j == n_blocks - 1, 2, 1))
        heads = range(part * hp, (part + 1) * hp)
        qg = jnp.concatenate(
            [q_ref[0, g * HEAD_DIM:(g + 1) * HEAD_DIM, pl.ds(q0, SWA_QB)] for g in heads], axis=1)
        qm = _mask_rows_to_head(jnp.concatenate([qg, qg], axis=0), kv_in_pair)
        mask = mask_ref[variant]
        s_loc = (jnp.dot(k_ref[0, 0, pl.ds(k0, SWA_KB), :], qm, preferred_element_type=F32)
                 + jnp.concatenate([mask] * hp, axis=1))
        s_ctx = jnp.dot(kc_ref[0, 0], qm, preferred_element_type=F32)
        sl_ref[slot] = s_loc
        sc_ref[slot] = s_ctx
        m_ref[slot] = jnp.maximum(_col_max(s_loc, s_ctx), sink_ref[0, :, part * hp * SWA_QB:(part + 1) * hp * SWA_QB])

    def attend(j, part, slot):
        k0, q0 = window(j)
        m = m_ref[slot]
        sink = sink_ref[0, :, part * hp * SWA_QB:(part + 1) * hp * SWA_QB]
        p_loc = jnp.exp2(sl_ref[slot] - m).astype(BF16)
        p_ctx = jnp.exp2(sc_ref[slot] - m).astype(BF16)
        v_loc = jnp.concatenate([v_ref[0, :, pl.ds(k0, SWA_KB)], ones_loc], axis=0)
        v_ctx = jnp.concatenate([vc_ref[0], ones_ctx], axis=0)
        o = (jnp.dot(v_loc, p_loc, preferred_element_type=F32)
             + jnp.dot(v_ctx, p_ctx, preferred_element_type=F32))
        o = o[:HEAD_DIM] / (o[HEAD_DIM:HEAD_DIM + 1] + jnp.exp2(sink - m))
        o_rows = jnp.concatenate([o[:, g * SWA_QB:(g + 1) * SWA_QB] for g in range(hp)], axis=0)
        o_ref[0, part, pl.ds(q0, SWA_QB), :] = o_rows.T.astype(BF16)

    for n in range(PIPE_AHEAD):
        scores(n // parts, n % parts, n)

    def body(it, carry):
        base = it * unroll
        for n in range(unroll * parts):
            ahead = n + PIPE_AHEAD
            scores(base + ahead // parts, ahead % parts, ahead % PIPE_SLOTS)
            attend(base + n // parts, n % parts, n % PIPE_SLOTS)
        return carry

    lax.fori_loop(0, n_blocks // unroll, body, 0)


def _swa_mask_tables():
    kr = np.arange(SWA_KB)[:, None]
    qc = np.arange(SWA_QB)[None, :]
    rel = [kr - qc, kr - SWA_WINDOW - qc, kr - (SWA_KB - SWA_QB) - qc]
    return np.stack([np.where(np.abs(r) <= SWA_WINDOW, 0.0, NEG_INF) for r in rel]).astype(np.float32)


def _swa_attention(qt, k, vt, kc, vct, sink_rows, unroll=16):
    b, d, l = qt.shape
    c = vct.shape[2] // b
    kv = vt.shape[1] // HEAD_DIM
    gw = SWA_GROUP * HEAD_DIM
    parts = SWA_GROUP // SWA_UNIT_HEADS
    n_blocks = l // SWA_QB
    assert SWA_UNIT_HEADS * HEAD_DIM == HEAD_PAIR
    assert l >= SWA_KB and n_blocks >= 3 and n_blocks % unroll == 0
    assert (unroll * SWA_GROUP // SWA_UNIT_HEADS) % PIPE_SLOTS == 0 and PIPE_SLOTS > PIPE_AHEAD
    return pl.pallas_call(
        functools.partial(_swa_kernel, n_blocks=n_blocks, seq=l, unroll=unroll),
        grid=(b, kv),
        in_specs=[
            pl.BlockSpec((1, gw, l), lambda i, n: (i, n, 0)),
            pl.BlockSpec((1, 1, l, HEAD_PAIR), lambda i, n: (i, n // 2, 0, 0)),
            pl.BlockSpec((1, HEAD_DIM, l), lambda i, n: (i, n, 0)),
            pl.BlockSpec((1, 1, c, HEAD_PAIR), lambda i, n: (0, n // 2, i, 0)),
            pl.BlockSpec((1, HEAD_DIM, c), lambda i, n: (0, n, i)),
            _const_spec((3, SWA_KB, SWA_QB)),
            pl.BlockSpec((1, 1, SWA_GROUP * SWA_QB), lambda i, n: (n, 0, 0)),
        ],
        out_specs=pl.BlockSpec((1, parts, l, HEAD_PAIR), lambda i, n: (i, n, 0, 0)),
        out_shape=jax.ShapeDtypeStruct((b, d // HEAD_PAIR, l, HEAD_PAIR), BF16),
        scratch_shapes=[
            pltpu.VMEM((PIPE_SLOTS, SWA_KB, SWA_UNIT_HEADS * SWA_QB), F32),
            pltpu.VMEM((PIPE_SLOTS, c, SWA_UNIT_HEADS * SWA_QB), F32),
            pltpu.VMEM((PIPE_SLOTS, 1, SWA_UNIT_HEADS * SWA_QB), F32),
        ],
        compiler_params=_params("arbitrary", "arbitrary"),
        name="sliding_window_attention",
    )(qt, k, vt, kc, vct, jnp.asarray(_swa_mask_tables()), sink_rows)


def _post_kernel(a_ref, h_ref, mod_ref, g_ref, wo_ref, w1_ref, w2_ref, o_ref, *, ff_chunk):
    mod = mod_ref[0]
    a = jnp.concatenate([a_ref[0, p] for p in range(a_ref.shape[1])], axis=1)
    y = jnp.dot(a, wo_ref[...], preferred_element_type=F32)
    h1 = h_ref[0] + mod[2:3] * y
    u = _rms_modulate(h1, g_ref[...], mod[3:4], mod[4:5]).astype(BF16)
    d_ff = w1_ref.shape[1]
    acc = jnp.zeros_like(h1)
    for c0 in range(0, d_ff, ff_chunk):
        t = jnp.dot(u, w1_ref[:, c0:c0 + ff_chunk], preferred_element_type=F32)
        r = jnp.square(jnp.maximum(t, 0.0)).astype(BF16)
        acc = acc + jnp.dot(r, w2_ref[c0:c0 + ff_chunk, :], preferred_element_type=F32)
    o_ref[0] = h1 + mod[5:6] * acc


def _post_block(a, h, mod, g, wo, w1, w2, *, tm, ff_chunk=1024):
    b, t, d = h.shape
    return pl.pallas_call(
        functools.partial(_post_kernel, ff_chunk=ff_chunk),
        grid=(b, t // tm),
        in_specs=[
            pl.BlockSpec((1, d // HEAD_PAIR, tm, HEAD_PAIR), lambda i, j: (i, 0, j, 0)),
            pl.BlockSpec((1, tm, d), lambda i, j: (i, j, 0)),
            pl.BlockSpec((1, 6, d), lambda i, j: (i, 0, 0)),
            _const_spec((1, d)),
            _const_spec(wo.shape),
            _const_spec(w1.shape),
            _const_spec(w2.shape),
        ],
        out_specs=pl.BlockSpec((1, tm, d), lambda i, j: (i, j, 0)),
        out_shape=jax.ShapeDtypeStruct((b, t, d), F32),
        compiler_params=_params("arbitrary", "arbitrary"),
        name="outproj_mlp",
    )(a, h, mod, g, wo, w1, w2)


def _rope_tables_t(l):
    t = np.arange(l)
    n_freq = HEAD_DIM // 4
    inv = ROPE_BASE ** (-np.arange(n_freq, dtype=np.float64) / n_freq)
    ang = np.stack([(t // GRID_W)[:, None] * inv, (t % GRID_W)[:, None] * inv], axis=1)
    cos = np.transpose(np.cos(ang), (1, 2, 0))
    sin = np.transpose(np.sin(ang), (1, 2, 0))
    cos_t = np.stack([cos, cos], axis=1).reshape(HEAD_DIM, l)
    sin_t = np.stack([-sin, sin], axis=1).reshape(HEAD_DIM, l)
    return jnp.asarray(cos_t, F32), jnp.asarray(sin_t, F32)


def _swap_rotary_halves(v):
    q = HEAD_DIM // 4
    return jnp.concatenate([v[q:2 * q], v[0:q], v[3 * q:4 * q], v[2 * q:3 * q]], axis=0)


def _gain_rope_tables(gain, cos_t, sin_t):
    gain = gain.astype(F32)
    return ((gain[:, None] * cos_t).astype(BF16), (_swap_rotary_halves(gain)[:, None] * sin_t).astype(BF16))


def _gain_cols(gain, scale=1.0):
    return jnp.broadcast_to((gain.astype(F32) * scale)[:, None], (HEAD_DIM, QKV_SUB))


def kernel(x, c, ctx, c_ctx, ada_w, ada_b, g_mix, g_mlp, mlp_w1, mlp_w2, na_wqkv, na_q_gain, na_k_gain,
           na_rpb, na_wo, swa_wqkv, swa_q_gain, swa_k_gain, swa_sink, swa_wo):
    b, l, d = x.shape
    n_ctx = ctx.shape[1]
    tm = TM_POST
    tmc = b * n_ctx
    ctx_flat = ctx.reshape(1, tmc, d)
    scale = HEAD_DIM ** -0.5 * LOG2E

    cond =jnp.concatenate([c, c_ctx[None], jnp.zeros((8 - b - 1, d), F32)], axis=0)
    mods = _modulation(cond, ada_w, ada_b)

    def mods_of(i):
        lat = mods[i, :b].reshape(b, 6, d)
        cx = mods[i, b].reshape(1, 6, d)
        return lat, cx

    def row(v):
        return v.reshape(1, d)

    mod_lat, mod_ctx = mods_of(0)
    gq, gk = na_q_gain[0], na_k_gain[0]
    gains = (_gain_cols(gq, scale), _gain_cols(gk))
    qt, k, vt, wt = _qkv_project(x, mod_lat, row(g_mix[0]), na_wqkv[0], gains, None, kv_width=d, tm=TM_QKV)
    qct, kc, vct = _qkv_project(ctx_flat, mod_ctx, row(g_mix[0]), wt, gains, None, kv_width=d, tm=tmc)
    a_lat = _na_attention(qt, k, vt, kc, vct, _na_bias_rows(na_rpb[0] * LOG2E))
    a_ctx = _ctx_attention(qct, kc, vct, b)
    wo = na_wo[0].astype(BF16)
    w1 = mlp_w1[0].astype(BF16)
    w2 = mlp_w2[0].astype(BF16)
    h_lat = _post_block(a_lat, x, mod_lat, row(g_mlp[0]), wo, w1, w2, tm=tm)
    h_ctx = _post_block(a_ctx, ctx_flat, mod_ctx, row(g_mlp[0]), wo, w1, w2, tm=tmc)

    mod_lat, mod_ctx = mods_of(1)
    kvw = SWA_KV_HEADS * HEAD_DIM
    gq, gk = swa_q_gain[0], swa_k_gain[0]
    cos_t, sin_t = _rope_tables_t(l)
    rope_tabs = (_gain_rope_tables(gq * scale, cos_t, sin_t) + _gain_rope_tables(gk, cos_t, sin_t))
    qt, k, vt, wt = _qkv_project(h_lat, mod_lat, row(g_mix[1]), swa_wqkv[0], None, rope_tabs, kv_width=kvw,
                                 tm=TM_QKV)
    gains = (_gain_cols(gq, scale), _gain_cols(gk))
    _, kc, vct = _qkv_project(h_ctx, mod_ctx, row(g_mix[1]), wt, gains, None, kv_width=kvw, tm=tmc)
    sink_rows = jnp.repeat(swa_sink[0].astype(F32) * LOG2E, SWA_QB).reshape(SWA_KV_HEADS, 1, SWA_GROUP * SWA_QB)
    a_lat = _swa_attention(qt, k, vt, kc, vct, sink_rows)
    return _post_block(a_lat, h_lat, mod_lat, row(g_mlp[1]), swa_wo[0].astype(BF16),
                       mlp_w1[1].astype(BF16), mlp_w2[1].astype(BF16), tm=tm)
```

```python
import functools

import numpy as np
import jax
import jax.numpy as jnp
from jax import lax
from jax.experimental import pallas as pl
from jax.experimental.pallas import tpu as pltpu

GRID_W = 64
N_HEADS = 16
HEAD_DIM = 64
NA_KH = 8
NA_KW = 16
SWA_KV_HEADS = 4
SWA_GROUP = N_HEADS // SWA_KV_HEADS
SWA_WINDOW = 128
ROPE_BASE = 10000.0
NORM_EPS = 1e-6
NEG_INF = -1e30
LOG2E = float(np.log2(np.e))

F32 = jnp.float32
BF16 = jnp.bfloat16

V7X_VMEM_BYTES = 64 * 1024 * 1024
VMEM_LIMIT_BYTES = V7X_VMEM_BYTES - 8 * 1024 * 1024

BF16_SUBLANES = 16
HEAD_PAIR = 2 * HEAD_DIM
NA_Q_ROWS = 4
NA_K_ROWS = NA_Q_ROWS + NA_KH
NA_QB = NA_Q_ROWS * GRID_W
NA_KB = NA_K_ROWS * GRID_W
SWA_QB = 128
SWA_KB = SWA_QB + 2 * SWA_WINDOW
SWA_UNIT_HEADS = 2
TM_QKV = 1024
TM_POST = 1024
QKV_SUB = 256
QKV_CHUNK = 256
QKV_CHUNK_ROPE = 1024
PIPE_AHEAD = 2
SWA_PIPE_AHEAD = 3
PIPE_SLOTS = 4


def _params(*semantics):
    return pltpu.CompilerParams(dimension_semantics=semantics, vmem_limit_bytes=VMEM_LIMIT_BYTES)


def _const_spec(shape):
    zeros = (0,) * len(shape)
    return pl.BlockSpec(shape, lambda *_: zeros, pipeline_mode=pl.Buffered(1))


def _mod_kernel(cond_ref, w_ref, b_ref, o_ref):
    s = jax.nn.silu(cond_ref[...]).astype(BF16)
    o_ref[0] = jnp.dot(s, w_ref[0].astype(BF16), preferred_element_type=F32) + b_ref[0]


def _modulation(cond, ada_w, ada_b, tn=1536):
    depth, d, n = ada_w.shape
    rows = cond.shape[0]
    return pl.pallas_call(
        _mod_kernel,
        grid=(depth, n // tn),
        in_specs=[
            pl.BlockSpec((rows, d), lambda i, j: (0, 0)),
            pl.BlockSpec((1, d, tn), lambda i, j: (i, 0, j)),
            pl.BlockSpec((1, 1, tn), lambda i, j: (i, 0, j)),
        ],
        out_specs=pl.BlockSpec((1, rows, tn), lambda i, j: (i, 0, j)),
        out_shape=jax.ShapeDtypeStruct((depth, rows, n), F32),
        compiler_params=_params("arbitrary", "arbitrary"),
        name="adaln_modulation",
    )(cond, ada_w, ada_b.reshape(depth, 1, n))


def _rms_modulate(x, g, shift, scale):
    y = x * lax.rsqrt(jnp.mean(x * x, axis=-1, keepdims=True) + NORM_EPS) * g
    return y * (1.0 + scale) + shift


def _head_norm_t(a, gain_b):
    n = a.shape[0] // HEAD_DIM
    tm = a.shape[1]
    a = a.reshape(n, HEAD_DIM, tm)
    a = (a * lax.rsqrt(jnp.mean(a * a, axis=1, keepdims=True) + NORM_EPS)).astype(BF16) * gain_b[None]
    return a.reshape(n * HEAD_DIM, tm)


def _head_norm_rope_t(a, gain_cos, gain_sin):
    n = a.shape[0] // HEAD_DIM
    tm = a.shape[1]
    a = a.reshape(n, HEAD_DIM, tm)
    a = (a * lax.rsqrt(jnp.mean(a * a, axis=1, keepdims=True) + NORM_EPS)).astype(BF16)
    q = HEAD_DIM // 4
    swapped = jnp.concatenate([a[:, q:2 * q], a[:, 0:q], a[:, 3 * q:4 * q], a[:, 2 * q:3 * q]], axis=1)
    a = a * gain_cos[None] + swapped * gain_sin[None]
    return a.reshape(n * HEAD_DIM, tm)


def _qkv_kernel(*refs, d_model, kv_width, use_rope, chunk, sub, prepare_weight):
    if prepare_weight:
        *refs, wt_ref = refs
    if use_rope:
        h_ref, mod_ref, g_ref, w_ref, qc_ref, qs_ref, kc_ref, ks_ref, q_ref, k_ref, v_ref = refs
    else:
        h_ref, mod_ref, g_ref, w_ref, gq_ref, gk_ref, q_ref, k_ref, v_ref = refs
        gq = gq_ref[...].astype(BF16)
        gk = gk_ref[...].astype(BF16)
    if prepare_weight:
        @pl.when((pl.program_id(0) == 0) & (pl.program_id(1) == 0))
        def _():
            for c0 in range(0, wt_ref.shape[0], QKV_SUB):
                wt_ref[c0:c0 + QKV_SUB, :] = w_ref[:, c0:c0 + QKV_SUB].T.astype(BF16)
    else:
        wt_ref = w_ref
    mod = mod_ref[0]
    g_eff = (g_ref[...] * (1.0 + mod[1:2])).astype(BF16)
    shift = mod[0:1].astype(BF16)
    contract_last = (((1,), (1,)), ((), ()))
    kv_chunk = min(chunk, kv_width)
    for t0 in range(0, h_ref.shape[1], sub):
        toks = slice(t0, t0 + sub)
        x = h_ref[0, toks, :]
        xn = (x * lax.rsqrt(jnp.mean(x * x, axis=-1, keepdims=True) + NORM_EPS)).astype(BF16)
        u = xn * g_eff + shift

        def proj(r0, rows):
            return lax.dot_general(wt_ref[r0:r0 + rows, :], u, contract_last,
                                   preferred_element_type=F32)

        def norm_q(a):
            if use_rope:
                return _head_norm_rope_t(a, qc_ref[:, toks], qs_ref[:, toks])
            return _head_norm_t(a, gq)

        def norm_k(a):
            if use_rope:
                return _head_norm_rope_t(a, kc_ref[:, toks], ks_ref[:, toks])
            return _head_norm_t(a, gk)

        for r0 in range(0, d_model, chunk):
            q_ref[0, r0:r0 + chunk, toks] = norm_q(proj(r0, chunk))
        for r0 in range(0, kv_width, kv_chunk):
            kt = norm_k(proj(d_model + r0, kv_chunk)).T
            for c0 in range(0, kv_chunk, HEAD_PAIR):
                k_ref[0, (r0 + c0) // HEAD_PAIR, toks, :] = kt[:, c0:c0 + HEAD_PAIR]
        for r0 in range(0, kv_width, kv_chunk):
            v_ref[0, r0:r0 + kv_chunk, toks] = proj(d_model + kv_width + r0, kv_chunk).astype(BF16)


def _qkv_project(h, mod, g, w, gains, rope_tabs, *, kv_width, tm, sub=QKV_SUB):
    b, t, d = h.shape
    use_rope = rope_tabs is not None
    prepare_weight = w.dtype == F32
    n_out = d + 2 * kv_width
    assert w.shape == ((d, n_out) if prepare_weight else (n_out, d))
    chunk = QKV_CHUNK_ROPE if use_rope else QKV_CHUNK
    in_specs = [
        pl.BlockSpec((1, tm, d), lambda i, j: (i, j, 0)),
        pl.BlockSpec((1, 6, d), lambda i, j: (i, 0, 0)),
        _const_spec((1, d)),
        _const_spec(w.shape),
    ]
    args = [h, mod, g, w]
    if use_rope:
        in_specs += [pl.BlockSpec((HEAD_DIM, tm), lambda i, j: (0, j))] * 4
        args += list(rope_tabs)
    else:
        in_specs += [_const_spec((HEAD_DIM, sub))] * 2
        args += list(gains)
    out_specs = [
        pl.BlockSpec((1, d, tm), lambda i, j: (i, 0, j)),
        pl.BlockSpec((1, kv_width // HEAD_PAIR, tm, HEAD_PAIR), lambda i, j: (i, 0, j, 0)),
        pl.BlockSpec((1, kv_width, tm), lambda i, j: (i, 0, j)),
    ]
    out_shape = [
        jax.ShapeDtypeStruct((b, d, t), BF16),
        jax.ShapeDtypeStruct((b, kv_width // HEAD_PAIR, t, HEAD_PAIR), BF16),
        jax.ShapeDtypeStruct((b, kv_width, t), BF16),
    ]
    if prepare_weight:
        out_specs.append(_const_spec((n_out, d)))
        out_shape.append(jax.ShapeDtypeStruct((n_out, d), BF16))
    return pl.pallas_call(
        functools.partial(_qkv_kernel, d_model=d, kv_width=kv_width, use_rope=use_rope, chunk=chunk,
                          sub=sub, prepare_weight=prepare_weight),
        grid=(b, t // tm),
        in_specs=in_specs,
        out_specs=out_specs,
        out_shape=out_shape,
        compiler_params=_params("arbitrary", "arbitrary"),
        name="qkv_rope" if use_rope else "qkv",
    )(*args)


def _col_max(*parts):
    m = parts[0].max(axis=0, keepdims=True)
    for p in parts[1:]:
        m = jnp.maximum(m, p.max(axis=0, keepdims=True))
    return m


def _mask_rows_to_head(q_pair, head):
    row = lax.broadcasted_iota(jnp.int32, q_pair.shape, 0)
    return jnp.where((row // HEAD_DIM) == head, q_pair, jnp.zeros_like(q_pair))


def _na_build_bias(rows_ref, bias_ref):
    w = GRID_W
    kc = lax.broadcasted_iota(jnp.int32, (w, 2 * w), 0)
    qc = lax.broadcasted_iota(jnp.int32, (w, 2 * w), 1) % w
    c0 = jnp.clip(qc - NA_KW // 2, 0, w - NA_KW)
    col_valid = (kc >= c0) & (kc < c0 + NA_KW)
    plan, combos = _na_bias_plan()
    for hh in range(2):
        tiles = []
        for n in range(len(combos)):
            row = jnp.broadcast_to(rows_ref[hh, n:n + 1, :], (w, 2 * w))
            tiles.append(jnp.where(col_valid, pltpu.roll(row, 0, 1, stride=1, stride_axis=0), NEG_INF))
        for v in range(3):
            for kr in range(NA_K_ROWS):
                for half in range(NA_Q_ROWS // 2):
                    bias_ref[hh, v, kr * w:(kr + 1) * w, half * 2 * w:(half + 1) * 2 * w] = tiles[plan[v, kr, half]]


def _na_kernel(q_ref, k_ref, v_ref, kc_ref, vc_ref, rows_ref, o_ref, bias_ref, sl_ref, sc_ref, m_ref, *,
               n_blocks, grid_rows, unroll):
    @pl.when(pl.program_id(1) == 0)
    def _():
        _na_build_bias(rows_ref, bias_ref)

    ones_loc = jnp.ones((BF16_SUBLANES, NA_KB), BF16)
    ones_ctx = jnp.ones((BF16_SUBLANES, kc_ref.shape[2]), BF16)

    def window(i):
        start_row = jnp.clip(NA_Q_ROWS * i - NA_KH // 2, 0, grid_rows - NA_K_ROWS)
        return pl.multiple_of(start_row * GRID_W, NA_QB), pl.multiple_of(i * NA_QB, NA_QB)

    def scores(i, hh, slot):
        i = jnp.minimum(i, n_blocks - 1)
        k0, q0 = window(i)
        variant = jnp.where(i == 0, 0, jnp.where(i == n_blocks - 1, 2, 1))
        qm = _mask_rows_to_head(q_ref[0, :, pl.ds(q0, NA_QB)], hh)
        s_loc = (jnp.dot(k_ref[0, 0, pl.ds(k0, NA_KB), :], qm, preferred_element_type=F32)
                 + bias_ref[hh, variant])
        s_ctx = jnp.dot(kc_ref[0, 0], qm, preferred_element_type=F32)
        sl_ref[slot] = s_loc
        sc_ref[slot] = s_ctx
        m_ref[slot] = _col_max(s_loc, s_ctx)

    def attend(i, hh, slot):
        k0, _ = window(i)
        m = m_ref[slot]
        p_loc = jnp.exp2(sl_ref[slot] - m).astype(BF16)
        p_ctx = jnp.exp2(sc_ref[slot] - m).astype(BF16)
        rows = slice(hh * HEAD_DIM, (hh + 1) * HEAD_DIM)
        v_loc = jnp.concatenate([v_ref[0, rows, pl.ds(k0, NA_KB)], ones_loc], axis=0)
        v_ctx = jnp.concatenate([vc_ref[0, rows, :], ones_ctx], axis=0)
        o = (jnp.dot(v_loc, p_loc, preferred_element_type=F32)
             + jnp.dot(v_ctx, p_ctx, preferred_element_type=F32))
        return o[:HEAD_DIM] / o[HEAD_DIM:HEAD_DIM + 1]

    for n in range(PIPE_AHEAD):
        scores(n // 2, n % 2, n)

    single_trip = n_blocks == unroll

    def body(it, carry):
        base = it * unroll
        units = [(base + blk, hh) for blk in range(unroll + (PIPE_AHEAD + 1) // 2) for hh in range(2)]
        pair = []
        for n, (i, hh) in enumerate(units[:2 * unroll]):
            ahead = units[n + PIPE_AHEAD]
            if not (single_trip and ahead[0] >= n_blocks):
                scores(ahead[0], ahead[1], (n + PIPE_AHEAD) % PIPE_SLOTS)
            pair.append(attend(i, hh, n % PIPE_SLOTS))
            if hh == 1:
                q0 = pl.multiple_of(i * NA_QB, NA_QB)
                o_ref[0, 0, pl.ds(q0, NA_QB), :] = jnp.concatenate(pair, axis=0).T.astype(BF16)
                pair = []
        return carry

    if single_trip:
        body(0, 0)
    else:
        lax.fori_loop(0, n_blocks // unroll, body, 0)


def _na_attention(qt, k, vt, kc, vct, bias_rows, unroll=16):
    b, d, l = qt.shape
    c = vct.shape[2] // b
    pairs = d // HEAD_PAIR
    grid_rows = l // GRID_W
    n_blocks = grid_rows // NA_Q_ROWS
    n_combos = bias_rows.shape[1]
    assert grid_rows >= NA_K_ROWS and n_blocks >= 3 and n_blocks % unroll == 0
    assert (2 * unroll) % PIPE_SLOTS == 0 and PIPE_SLOTS > PIPE_AHEAD
    return pl.pallas_call(
        functools.partial(_na_kernel, n_blocks=n_blocks, grid_rows=grid_rows, unroll=unroll),
        grid=(pairs, b),
        in_specs=[
            pl.BlockSpec((1, HEAD_PAIR, l), lambda p, i: (i, p, 0)),
            pl.BlockSpec((1, 1, l, HEAD_PAIR), lambda p, i: (i, p, 0, 0)),
            pl.BlockSpec((1, HEAD_PAIR, l), lambda p, i: (i, p, 0)),
            pl.BlockSpec((1, 1, c, HEAD_PAIR), lambda p, i: (0, p, i, 0)),
            pl.BlockSpec((1, HEAD_PAIR, c), lambda p, i: (0, p, i)),
            pl.BlockSpec((2, n_combos, 2 * GRID_W), lambda p, i: (p, 0, 0)),
        ],
        out_specs=pl.BlockSpec((1, 1, l, HEAD_PAIR), lambda p, i: (i, p, 0, 0)),
        out_shape=jax.ShapeDtypeStruct((b, pairs, l, HEAD_PAIR), BF16),
        scratch_shapes=[
            pltpu.VMEM((2, 3, NA_KB, NA_QB), F32),
            pltpu.VMEM((PIPE_SLOTS, NA_KB, NA_QB), F32),
            pltpu.VMEM((PIPE_SLOTS, c, NA_QB), F32),
            pltpu.VMEM((PIPE_SLOTS, 1, NA_QB), F32),
        ],
        compiler_params=_params("arbitrary", "arbitrary"),
        name="neighbourhood_attention",
    )(qt, k, vt, kc, vct, bias_rows)


NA_MASKED_ROW = 2 * NA_KH - 1


@functools.lru_cache(maxsize=None)
def _na_bias_plan():
    kr = np.arange(NA_K_ROWS)[:, None]
    qr = np.arange(NA_Q_ROWS)[None, :]
    top = NA_KH - 1
    variants = [
        (kr < NA_KH + 0 * qr, kr - qr + top),
        ((kr >= qr) & (kr < qr + NA_KH), kr - qr + top - NA_KH // 2),
        (kr >= NA_K_ROWS - NA_KH + 0 * qr, kr - qr + top - (NA_K_ROWS - NA_Q_ROWS)),
    ]
    idx = np.stack([np.where(valid, off, NA_MASKED_ROW) for valid, off in variants])
    assert idx.min() >= 0 and idx.max() <= NA_MASKED_ROW
    combos = []
    plan = np.zeros((3, NA_K_ROWS, NA_Q_ROWS // 2), np.int32)
    for v in range(3):
        for r in range(NA_K_ROWS):
            for half in range(NA_Q_ROWS // 2):
                key = (int(idx[v, r, 2 * half]), int(idx[v, r, 2 * half + 1]))
                if key not in combos:
                    combos.append(key)
                plan[v, r, half] = combos.index(key)
    return plan, tuple(combos)


def _na_bias_rows(rpb):
    h, n_rows, n_off = rpb.shape
    w = GRID_W
    reach = NA_KW - 1
    _, combos = _na_bias_plan()
    masked = n_rows * n_off
    lane = np.arange(2 * w)
    first = np.where(lane <= reach, reach - lane, np.where(lane >= 2 * w - reach, reach + 2 * w - lane, -1))
    second = np.where(np.abs(lane - w) <= reach, reach + w - lane, -1)
    idx = np.full((len(combos), 2 * w), masked, np.int32)
    for n, (a, b) in enumerate(combos):
        if a != NA_MASKED_ROW:
            idx[n] = np.where(first >= 0, a * n_off + first, idx[n])
        if b != NA_MASKED_ROW:
            idx[n] = np.where(second >= 0, b * n_off + second, idx[n])
    flat = jnp.concatenate([rpb.reshape(h, masked), jnp.full((h, 1), NEG_INF, rpb.dtype)], axis=1)
    return flat[:, idx]


def _ctx_kernel(q_ref, k_ref, v_ref, o_ref, *, pairs):
    def scores(p, hh):
        cols = slice(p * HEAD_PAIR, (p + 1) * HEAD_PAIR)
        return jnp.dot(k_ref[0, p], _mask_rows_to_head(q_ref[0, cols, :], hh),
                       preferred_element_type=F32)

    heads = [(p, hh) for p in range(pairs) for hh in range(2)]
    s_next = scores(*heads[0])
    outs = []
    for n, (p, hh) in enumerate(heads):
        s = s_next
        if n + 1 < len(heads):
            s_next = scores(*heads[n + 1])
        pr = jnp.exp2(s - s.max(axis=0, keepdims=True))
        l = pr.sum(axis=0, keepdims=True)
        rows = slice(p * HEAD_PAIR + hh * HEAD_DIM, p * HEAD_PAIR + (hh + 1) * HEAD_DIM)
        outs.append(jnp.dot(v_ref[0, rows, :], pr.astype(BF16), preferred_element_type=F32) / l)
        if hh == 1:
            o_ref[0, p] = jnp.concatenate(outs, axis=0).T.astype(BF16)
            outs = []


def _ctx_attention(qct, kc, vct, batches):
    _, d, bc = qct.shape
    c = bc // batches
    pairs = d // HEAD_PAIR
    return pl.pallas_call(
        functools.partial(_ctx_kernel, pairs=pairs),
        grid=(batches,),
        in_specs=[
            pl.BlockSpec((1, d, c), lambda i: (0, 0, i)),
            pl.BlockSpec((1, pairs, c, HEAD_PAIR), lambda i: (0, 0, i, 0)),
            pl.BlockSpec((1, d, c), lambda i: (0, 0, i)),
        ],
        out_specs=pl.BlockSpec((1, pairs, c, HEAD_PAIR), lambda i: (0, 0, i, 0)),
        out_shape=jax.ShapeDtypeStruct((1, pairs, bc, HEAD_PAIR), BF16),
        compiler_params=_params("arbitrary"),
        name="context_attention",
    )(qct, kc, vct)


def _swa_kernel(q_ref, k_ref, v_ref, kc_ref, vc_ref, mask_ref, sink_ref, o_ref, sl_ref, sc_ref, m_ref, *,
                n_blocks, seq, unroll):
    kv_in_pair = pl.program_id(1) % 2
    ones_loc = jnp.ones((BF16_SUBLANES, SWA_KB), BF16)
    ones_ctx = jnp.ones((BF16_SUBLANES, kc_ref.shape[2]), BF16)

    def window(j):
        k0 = pl.multiple_of(jnp.clip(SWA_QB * j - SWA_WINDOW, 0, seq - SWA_KB), SWA_QB)
        return k0, pl.multiple_of(j * SWA_QB, SWA_QB)

    hp = SWA_UNIT_HEADS
    parts = SWA_GROUP // hp

    def scores(j, part, slot):
        j = jnp.minimum(j, n_blocks - 1)
        k0, q0 = window(j)
        variant = jnp.where(j == 0, 0, jnp.where(j == n_blocks - 1, 2, 1))
        heads = range(part * hp, (part + 1) * hp)
        qg = jnp.concatenate(
            [q_ref[0, g * HEAD_DIM:(g + 1) * HEAD_DIM, pl.ds(q0, SWA_QB)] for g in heads], axis=1)
        qm = _mask_rows_to_head(jnp.concatenate([qg, qg], axis=0), kv_in_pair)
        mask = mask_ref[variant]
        s_loc = (jnp.dot(k_ref[0, 0, pl.ds(k0, SWA_KB), :], qm, preferred_element_type=F32)
                 + jnp.concatenate([mask] * hp, axis=1))
        s_ctx = jnp.dot(kc_ref[0, 0], qm, preferred_element_type=F32)
        sl_ref[slot] = s_loc
        sc_ref[slot] = s_ctx
        m_ref[slot] = jnp.maximum(_col_max(s_loc, s_ctx), sink_ref[0, :, part * hp * SWA_QB:(part + 1) * hp * SWA_QB])

    def attend(j, part, slot):
        k0, q0 = window(j)
        m = m_ref[slot]
        sink = sink_ref[0, :, part * hp * SWA_QB:(part + 1) * hp * SWA_QB]
        p_loc = jnp.exp2(sl_ref[slot] - m).astype(BF16)
        p_ctx = jnp.exp2(sc_ref[slot] - m).astype(BF16)
        v_loc = jnp.concatenate([v_ref[0, :, pl.ds(k0, SWA_KB)], ones_loc], axis=0)
        v_ctx = jnp.concatenate([vc_ref[0], ones_ctx], axis=0)
        o = (jnp.dot(v_loc, p_loc, preferred_element_type=F32)
             + jnp.dot(v_ctx, p_ctx, preferred_element_type=F32))
        o = o[:HEAD_DIM] / (o[HEAD_DIM:HEAD_DIM + 1] + jnp.exp2(sink - m))
        o_rows = jnp.concatenate([o[:, g * SWA_QB:(g + 1) * SWA_QB] for g in range(hp)], axis=0)
        o_ref[0, part, pl.ds(q0, SWA_QB), :] = o_rows.T.astype(BF16)

    for n in range(SWA_PIPE_AHEAD):
        scores(n // parts, n % parts, n)

    def body(it, carry):
        base = it * unroll
        for n in range(unroll * parts):
            ahead = n + SWA_PIPE_AHEAD
            scores(base + ahead // parts, ahead % parts, ahead % PIPE_SLOTS)
            attend(base + n // parts, n % parts, n % PIPE_SLOTS)
        return carry

    lax.fori_loop(0, n_blocks // unroll, body, 0)


def _swa_mask_tables():
    kr = np.arange(SWA_KB)[:, None]
    qc = np.arange(SWA_QB)[None, :]
    rel = [kr - qc, kr - SWA_WINDOW - qc, kr - (SWA_KB - SWA_QB) - qc]
    return np.stack([np.where(np.abs(r) <= SWA_WINDOW, 0.0, NEG_INF) for r in rel]).astype(np.float32)


def _swa_attention(qt, k, vt, kc, vct, sink_rows, unroll=32):
    b, d, l = qt.shape
    c = vct.shape[2] // b
    kv = vt.shape[1] // HEAD_DIM
    gw = SWA_GROUP * HEAD_DIM
    parts = SWA_GROUP // SWA_UNIT_HEADS
    n_blocks = l // SWA_QB
    assert SWA_UNIT_HEADS * HEAD_DIM == HEAD_PAIR
    assert l >= SWA_KB and n_blocks >= 3 and n_blocks % unroll == 0
    assert (unroll * SWA_GROUP // SWA_UNIT_HEADS) % PIPE_SLOTS == 0 and PIPE_SLOTS > SWA_PIPE_AHEAD
    return pl.pallas_call(
        functools.partial(_swa_kernel, n_blocks=n_blocks, seq=l, unroll=unroll),
        grid=(b, kv),
        in_specs=[
            pl.BlockSpec((1, gw, l), lambda i, n: (i, n, 0)),
            pl.BlockSpec((1, 1, l, HEAD_PAIR), lambda i, n: (i, n // 2, 0, 0)),
            pl.BlockSpec((1, HEAD_DIM, l), lambda i, n: (i, n, 0)),
            pl.BlockSpec((1, 1, c, HEAD_PAIR), lambda i, n: (0, n // 2, i, 0)),
            pl.BlockSpec((1, HEAD_DIM, c), lambda i, n: (0, n, i)),
            _const_spec((3, SWA_KB, SWA_QB)),
            pl.BlockSpec((1, 1, SWA_GROUP * SWA_QB), lambda i, n: (n, 0, 0)),
        ],
        out_specs=pl.BlockSpec((1, parts, l, HEAD_PAIR), lambda i, n: (i, n, 0, 0)),
        out_shape=jax.ShapeDtypeStruct((b, d // HEAD_PAIR, l, HEAD_PAIR), BF16),
        scratch_shapes=[
            pltpu.VMEM((PIPE_SLOTS, SWA_KB, SWA_UNIT_HEADS * SWA_QB), F32),
            pltpu.VMEM((PIPE_SLOTS, c, SWA_UNIT_HEADS * SWA_QB), F32),
            pltpu.VMEM((PIPE_SLOTS, 1, SWA_UNIT_HEADS * SWA_QB), F32),
        ],
        compiler_params=_params("arbitrary", "arbitrary"),
        name="sliding_window_attention",
    )(qt, k, vt, kc, vct, jnp.asarray(_swa_mask_tables()), sink_rows)


def _post_kernel(a_ref, h_ref, mod_ref, g_ref, wo_ref, w1_ref, w2_ref, o_ref, *, ff_chunk):
    mod = mod_ref[0]
    a = jnp.concatenate([a_ref[0, p] for p in range(a_ref.shape[1])], axis=1)
    y = jnp.dot(a, wo_ref[...], preferred_element_type=F32)
    h1 = h_ref[0] + mod[2:3] * y
    u = _rms_modulate(h1, g_ref[...], mod[3:4], mod[4:5]).astype(BF16)
    d_ff = w1_ref.shape[1]
    acc = jnp.zeros_like(h1)
    for c0 in range(0, d_ff, ff_chunk):
        t = jnp.dot(u, w1_ref[:, c0:c0 + ff_chunk], preferred_element_type=F32)
        r = jnp.square(jnp.maximum(t, 0.0)).astype(BF16)
        acc = acc + jnp.dot(r, w2_ref[c0:c0 + ff_chunk, :], preferred_element_type=F32)
    o_ref[0] = h1 + mod[5:6] * acc


def _post_block(a, h, mod, g, wo, w1, w2, *, tm, ff_chunk=1024):
    b, t, d = h.shape
    return pl.pallas_call(
        functools.partial(_post_kernel, ff_chunk=ff_chunk),
        grid=(b, t // tm),
        in_specs=[
            pl.BlockSpec((1, d // HEAD_PAIR, tm, HEAD_PAIR), lambda i, j: (i, 0, j, 0)),
            pl.BlockSpec((1, tm, d), lambda i, j: (i, j, 0)),
            pl.BlockSpec((1, 6, d), lambda i, j: (i, 0, 0)),
            _const_spec((1, d)),
            _const_spec(wo.shape),
            _const_spec(w1.shape),
            _const_spec(w2.shape),
        ],
        out_specs=pl.BlockSpec((1, tm, d), lambda i, j: (i, j, 0)),
        out_shape=jax.ShapeDtypeStruct((b, t, d), F32),
        compiler_params=_params("arbitrary", "arbitrary"),
        name="outproj_mlp",
    )(a, h, mod, g, wo, w1, w2)


def _rope_tables_t(l):
    t = np.arange(l)
    n_freq = HEAD_DIM // 4
    inv = ROPE_BASE ** (-np.arange(n_freq, dtype=np.float64) / n_freq)
    ang = np.stack([(t // GRID_W)[:, None] * inv, (t % GRID_W)[:, None] * inv], axis=1)
    cos = np.transpose(np.cos(ang), (1, 2, 0))
    sin = np.transpose(np.sin(ang), (1, 2, 0))
    cos_t = np.stack([cos, cos], axis=1).reshape(HEAD_DIM, l)
    sin_t = np.stack([-sin, sin], axis=1).reshape(HEAD_DIM, l)
    return jnp.asarray(cos_t, F32), jnp.asarray(sin_t, F32)


def _swap_rotary_halves(v):
    q = HEAD_DIM // 4
    return jnp.concatenate([v[q:2 * q], v[0:q], v[3 * q:4 * q], v[2 * q:3 * q]], axis=0)


def _gain_rope_tables(gain, cos_t, sin_t):
    gain = gain.astype(F32)
    return ((gain[:, None] * cos_t).astype(BF16), (_swap_rotary_halves(gain)[:, None] * sin_t).astype(BF16))


def _gain_cols(gain, scale=1.0):
    return jnp.broadcast_to((gain.astype(F32) * scale)[:, None], (HEAD_DIM, QKV_SUB))


def kernel(x, c, ctx, c_ctx, ada_w, ada_b, g_mix, g_mlp, mlp_w1, mlp_w2, na_wqkv, na_q_gain, na_k_gain,
           na_rpb, na_wo, swa_wqkv, swa_q_gain, swa_k_gain, swa_sink, swa_wo):
    b, l, d = x.shape
    n_ctx = ctx.shape[1]
    tm = TM_POST
    tmc = b * n_ctx
    ctx_flat = ctx.reshape(1, tmc, d)
    scale = HEAD_DIM ** -0.5 * LOG2E

    cond =jnp.concatenate([c, c_ctx[None], jnp.zeros((8 - b - 1, d), F32)], axis=0)
    mods = _modulation(cond, ada_w, ada_b)

    def mods_of(i):
        lat = mods[i, :b].reshape(b, 6, d)
        cx = mods[i, b].reshape(1, 6, d)
        return lat, cx

    def row(v):
        return v.reshape(1, d)

    mod_lat, mod_ctx = mods_of(0)
    gq, gk = na_q_gain[0], na_k_gain[0]
    gains = (_gain_cols(gq, scale), _gain_cols(gk))
    qt, k, vt, wt = _qkv_project(x, mod_lat, row(g_mix[0]), na_wqkv[0], gains, None, kv_width=d, tm=TM_QKV)
    qct, kc, vct = _qkv_project(ctx_flat, mod_ctx, row(g_mix[0]), wt, gains, None, kv_width=d, tm=tmc)
    a_lat = _na_attention(qt, k, vt, kc, vct, _na_bias_rows(na_rpb[0] * LOG2E))
    a_ctx = _ctx_attention(qct, kc, vct, b)
    wo = na_wo[0].astype(BF16)
    w1 = mlp_w1[0].astype(BF16)
    w2 = mlp_w2[0].astype(BF16)
    h_lat = _post_block(a_lat, x, mod_lat, row(g_mlp[0]), wo, w1, w2, tm=tm)
    h_ctx = _post_block(a_ctx, ctx_flat, mod_ctx, row(g_mlp[0]), wo, w1, w2, tm=tmc)

    mod_lat, mod_ctx = mods_of(1)
    kvw = SWA_KV_HEADS * HEAD_DIM
    gq, gk = swa_q_gain[0], swa_k_gain[0]
    cos_t, sin_t = _rope_tables_t(l)
    rope_tabs = (_gain_rope_tables(gq * scale, cos_t, sin_t) + _gain_rope_tables(gk, cos_t, sin_t))
    qt, k, vt, wt = _qkv_project(h_lat, mod_lat, row(g_mix[1]), swa_wqkv[0], None, rope_tabs, kv_width=kvw,
                                 tm=TM_QKV)
    gains = (_gain_cols(gq, scale), _gain_cols(gk))
    _, kc, vct = _qkv_project(h_ctx, mod_ctx, row(g_mix[1]), wt, gains, None, kv_width=kvw, tm=tmc)
    sink_rows = jnp.repeat(swa_sink[0].astype(F32) * LOG2E, SWA_QB).reshape(SWA_KV_HEADS, 1, SWA_GROUP * SWA_QB)
    a_lat = _swa_attention(qt, k, vt, kc, vct, sink_rows)
    return _post_block(a_lat, h_lat, mod_lat, row(g_mlp[1]), swa_wo[0].astype(BF16),
                       mlp_w1[1].astype(BF16), mlp_w2[1].astype(BF16), tm=tm)
```

```python
import functools

import numpy as np
import jax
import jax.numpy as jnp
from jax import lax
from jax.experimental import pallas as pl
from jax.experimental.pallas import tpu as pltpu

GRID_W = 64
N_HEADS = 16
HEAD_DIM = 64
NA_KH = 8
NA_KW = 16
SWA_KV_HEADS = 4
SWA_GROUP = N_HEADS // SWA_KV_HEADS
SWA_WINDOW = 128
ROPE_BASE = 10000.0
NORM_EPS = 1e-6
NEG_INF = -1e30
LOG2E = float(np.log2(np.e))

F32 = jnp.float32
BF16 = jnp.bfloat16

V7X_VMEM_BYTES = 64 * 1024 * 1024
VMEM_LIMIT_BYTES = V7X_VMEM_BYTES - 8 * 1024 * 1024

BF16_SUBLANES = 16
HEAD_PAIR = 2 * HEAD_DIM
NA_Q_ROWS = 4
NA_K_ROWS = NA_Q_ROWS + NA_KH
NA_QB = NA_Q_ROWS * GRID_W
NA_KB = NA_K_ROWS * GRID_W
SWA_QB = 128
SWA_KB = SWA_QB + 2 * SWA_WINDOW
SWA_UNIT_HEADS = 2
TM_QKV = 1024
TM_POST = 1024
QKV_SUB = 256
QKV_CHUNK = 256
QKV_CHUNK_ROPE = 1024
BATCHES_PER_STEP = 2
PIPE_AHEAD = 2
SWA_PIPE_AHEAD = 3
PIPE_SLOTS = 4


def _params(*semantics):
    return pltpu.CompilerParams(dimension_semantics=semantics, vmem_limit_bytes=VMEM_LIMIT_BYTES)


def _const_spec(shape):
    zeros = (0,) * len(shape)
    return pl.BlockSpec(shape, lambda *_: zeros, pipeline_mode=pl.Buffered(1))


def _mod_kernel(cond_ref, w_ref, b_ref, o_ref):
    s = jax.nn.silu(cond_ref[...]).astype(BF16)
    o_ref[0] = jnp.dot(s, w_ref[0].astype(BF16), preferred_element_type=F32) + b_ref[0]


def _modulation(cond, ada_w, ada_b, tn=1536):
    depth, d, n = ada_w.shape
    rows = cond.shape[0]
    return pl.pallas_call(
        _mod_kernel,
        grid=(depth, n // tn),
        in_specs=[
            pl.BlockSpec((rows, d), lambda i, j: (0, 0)),
            pl.BlockSpec((1, d, tn), lambda i, j: (i, 0, j)),
            pl.BlockSpec((1, 1, tn), lambda i, j: (i, 0, j)),
        ],
        out_specs=pl.BlockSpec((1, rows, tn), lambda i, j: (i, 0, j)),
        out_shape=jax.ShapeDtypeStruct((depth, rows, n), F32),
        compiler_params=_params("arbitrary", "arbitrary"),
        name="adaln_modulation",
    )(cond, ada_w, ada_b.reshape(depth, 1, n))


def _rms_modulate(x, g, shift, scale):
    y = x * lax.rsqrt(jnp.mean(x * x, axis=-1, keepdims=True) + NORM_EPS) * g
    return y * (1.0 + scale) + shift


def _head_norm_t(a, gain_b):
    n = a.shape[0] // HEAD_DIM
    tm = a.shape[1]
    a = a.reshape(n, HEAD_DIM, tm)
    a = (a * lax.rsqrt(jnp.mean(a * a, axis=1, keepdims=True) + NORM_EPS)).astype(BF16) * gain_b[None]
    return a.reshape(n * HEAD_DIM, tm)


def _head_norm_rope_t(a, gain_cos, gain_sin):
    n = a.shape[0] // HEAD_DIM
    tm = a.shape[1]
    a = a.reshape(n, HEAD_DIM, tm)
    a = (a * lax.rsqrt(jnp.mean(a * a, axis=1, keepdims=True) + NORM_EPS)).astype(BF16)
    q = HEAD_DIM // 4
    swapped = jnp.concatenate([a[:, q:2 * q], a[:, 0:q], a[:, 3 * q:4 * q], a[:, 2 * q:3 * q]], axis=1)
    a = a * gain_cos[None] + swapped * gain_sin[None]
    return a.reshape(n * HEAD_DIM, tm)


def _qkv_kernel(*refs, d_model, kv_width, use_rope, chunk, sub, prepare_weight):
    if prepare_weight:
        *refs, wt_ref = refs
    if use_rope:
        h_ref, mod_ref, g_ref, w_ref, qc_ref, qs_ref, kc_ref, ks_ref, q_ref, k_ref, v_ref = refs
    else:
        h_ref, mod_ref, g_ref, w_ref, gq_ref, gk_ref, q_ref, k_ref, v_ref = refs
        gq = gq_ref[...].astype(BF16)
        gk = gk_ref[...].astype(BF16)
    if prepare_weight:
        @pl.when((pl.program_id(0) == 0) & (pl.program_id(1) == 0))
        def _():
            for c0 in range(0, wt_ref.shape[0], QKV_SUB):
                wt_ref[c0:c0 + QKV_SUB, :] = w_ref[:, c0:c0 + QKV_SUB].T.astype(BF16)
    else:
        wt_ref = w_ref
    mod = mod_ref[0]
    g_eff = (g_ref[...] * (1.0 + mod[1:2])).astype(BF16)
    shift = mod[0:1].astype(BF16)
    contract_last = (((1,), (1,)), ((), ()))
    kv_chunk = min(chunk, kv_width)
    for t0 in range(0, h_ref.shape[1], sub):
        toks = slice(t0, t0 + sub)
        x = h_ref[0, toks, :]
        xn = (x * lax.rsqrt(jnp.mean(x * x, axis=-1, keepdims=True) + NORM_EPS)).astype(BF16)
        u = xn * g_eff + shift

        def proj(r0, rows):
            return lax.dot_general(wt_ref[r0:r0 + rows, :], u, contract_last,
                                   preferred_element_type=F32)

        def norm_q(a):
            if use_rope:
                return _head_norm_rope_t(a, qc_ref[:, toks], qs_ref[:, toks])
            return _head_norm_t(a, gq)

        def norm_k(a):
            if use_rope:
                return _head_norm_rope_t(a, kc_ref[:, toks], ks_ref[:, toks])
            return _head_norm_t(a, gk)

        for r0 in range(0, d_model, chunk):
            q_ref[0, r0:r0 + chunk, toks] = norm_q(proj(r0, chunk))
        for r0 in range(0, kv_width, kv_chunk):
            kt = norm_k(proj(d_model + r0, kv_chunk)).T
            for c0 in range(0, kv_chunk, HEAD_PAIR):
                k_ref[0, (r0 + c0) // HEAD_PAIR, toks, :] = kt[:, c0:c0 + HEAD_PAIR]
        for r0 in range(0, kv_width, kv_chunk):
            v_ref[0, r0:r0 + kv_chunk, toks] = proj(d_model + kv_width + r0, kv_chunk).astype(BF16)


def _qkv_project(h, mod, g, w, gains, rope_tabs, *, kv_width, tm, sub=QKV_SUB):
    b, t, d = h.shape
    use_rope = rope_tabs is not None
    prepare_weight = w.dtype == F32
    n_out = d + 2 * kv_width
    assert w.shape == ((d, n_out) if prepare_weight else (n_out, d))
    chunk = QKV_CHUNK_ROPE if use_rope else QKV_CHUNK
    in_specs = [
        pl.BlockSpec((1, tm, d), lambda i, j: (i, j, 0)),
        pl.BlockSpec((1, 6, d), lambda i, j: (i, 0, 0)),
        _const_spec((1, d)),
        _const_spec(w.shape),
    ]
    args = [h, mod, g, w]
    if use_rope:
        in_specs += [pl.BlockSpec((HEAD_DIM, tm), lambda i, j: (0, j))] * 4
        args += list(rope_tabs)
    else:
        in_specs += [_const_spec((HEAD_DIM, sub))] * 2
        args += list(gains)
    out_specs = [
        pl.BlockSpec((1, d, tm), lambda i, j: (i, 0, j)),
        pl.BlockSpec((1, kv_width // HEAD_PAIR, tm, HEAD_PAIR), lambda i, j: (i, 0, j, 0)),
        pl.BlockSpec((1, kv_width, tm), lambda i, j: (i, 0, j)),
    ]
    out_shape = [
        jax.ShapeDtypeStruct((b, d, t), BF16),
        jax.ShapeDtypeStruct((b, kv_width // HEAD_PAIR, t, HEAD_PAIR), BF16),
        jax.ShapeDtypeStruct((b, kv_width, t), BF16),
    ]
    if prepare_weight:
        out_specs.append(_const_spec((n_out, d)))
        out_shape.append(jax.ShapeDtypeStruct((n_out, d), BF16))
    return pl.pallas_call(
        functools.partial(_qkv_kernel, d_model=d, kv_width=kv_width, use_rope=use_rope, chunk=chunk,
                          sub=sub, prepare_weight=prepare_weight),
        grid=(b, t // tm),
        in_specs=in_specs,
        out_specs=out_specs,
        out_shape=out_shape,
        compiler_params=_params("arbitrary", "arbitrary"),
        name="qkv_rope" if use_rope else "qkv",
    )(*args)


def _col_max(*parts):
    m = parts[0].max(axis=0, keepdims=True)
    for p in parts[1:]:
        m = jnp.maximum(m, p.max(axis=0, keepdims=True))
    return m


def _mask_rows_to_head(q_pair, head):
    row = lax.broadcasted_iota(jnp.int32, q_pair.shape, 0)
    return jnp.where((row // HEAD_DIM) == head, q_pair, jnp.zeros_like(q_pair))


def _na_build_bias(rows_ref, bias_ref):
    w = GRID_W
    kc = lax.broadcasted_iota(jnp.int32, (w, 2 * w), 0)
    qc = lax.broadcasted_iota(jnp.int32, (w, 2 * w), 1) % w
    c0 = jnp.clip(qc - NA_KW // 2, 0, w - NA_KW)
    col_valid = (kc >= c0) & (kc < c0 + NA_KW)
    plan, combos = _na_bias_plan()
    for hh in range(2):
        tiles = []
        for n in range(len(combos)):
            row = jnp.broadcast_to(rows_ref[hh, n:n + 1, :], (w, 2 * w))
            tiles.append(jnp.where(col_valid, pltpu.roll(row, 0, 1, stride=1, stride_axis=0), NEG_INF))
        for v in range(3):
            for kr in range(NA_K_ROWS):
                for half in range(NA_Q_ROWS // 2):
                    bias_ref[hh, v, kr * w:(kr + 1) * w, half * 2 * w:(half + 1) * 2 * w] = tiles[plan[v, kr, half]]


def _na_kernel(q_ref, k_ref, v_ref, kc_ref, vc_ref, rows_ref, o_ref, bias_ref, sl_ref, sc_ref, m_ref, *,
               n_blocks, grid_rows, unroll):
    @pl.when(pl.program_id(1) == 0)
    def _():
        _na_build_bias(rows_ref, bias_ref)

    n_ctx = kc_ref.shape[2] // q_ref.shape[0]
    total_blocks = q_ref.shape[0] * n_blocks
    ones_loc = jnp.ones((BF16_SUBLANES, NA_KB), BF16)
    ones_ctx = jnp.ones((BF16_SUBLANES, n_ctx), BF16)

    def locate(g):
        bb = g // n_blocks
        i = g - bb * n_blocks
        start_row = jnp.clip(NA_Q_ROWS * i - NA_KH // 2, 0, grid_rows - NA_K_ROWS)
        return (bb, i, pl.multiple_of(start_row * GRID_W, NA_QB), pl.multiple_of(i * NA_QB, NA_QB),
                pl.multiple_of(bb * n_ctx, n_ctx))

    def scores(g, hh, slot):
        g = jnp.minimum(g, total_blocks - 1)
        bb, i, k0, q0, c0 = locate(g)
        variant = jnp.where(i == 0, 0, jnp.where(i == n_blocks - 1, 2, 1))
        qm = _mask_rows_to_head(q_ref[bb, :, pl.ds(q0, NA_QB)], hh)
        s_loc = (jnp.dot(k_ref[bb, 0, pl.ds(k0, NA_KB), :], qm, preferred_element_type=F32)
                 + bias_ref[hh, variant])
        s_ctx = jnp.dot(kc_ref[0, 0, pl.ds(c0, n_ctx), :], qm, preferred_element_type=F32)
        sl_ref[slot] = s_loc
        sc_ref[slot] = s_ctx
        m_ref[slot] = _col_max(s_loc, s_ctx)

    def attend(g, hh, slot):
        bb, _, k0, _, c0 = locate(g)
        m = m_ref[slot]
        p_loc = jnp.exp2(sl_ref[slot] - m).astype(BF16)
        p_ctx = jnp.exp2(sc_ref[slot] - m).astype(BF16)
        rows = slice(hh * HEAD_DIM, (hh + 1) * HEAD_DIM)
        v_loc = jnp.concatenate([v_ref[bb, rows, pl.ds(k0, NA_KB)], ones_loc], axis=0)
        v_ctx = jnp.concatenate([vc_ref[0, rows, pl.ds(c0, n_ctx)], ones_ctx], axis=0)
        o = (jnp.dot(v_loc, p_loc, preferred_element_type=F32)
             + jnp.dot(v_ctx, p_ctx, preferred_element_type=F32))
        return o[:HEAD_DIM] / o[HEAD_DIM:HEAD_DIM + 1]

    for n in range(PIPE_AHEAD):
        scores(n // 2, n % 2, n)

    def body(it, carry):
        base = it * unroll
        units = [(base + blk, hh) for blk in range(unroll + (PIPE_AHEAD + 1) // 2) for hh in range(2)]
        pair = []
        for n, (g, hh) in enumerate(units[:2 * unroll]):
            ahead = units[n + PIPE_AHEAD]
            scores(ahead[0], ahead[1], (n + PIPE_AHEAD) % PIPE_SLOTS)
            pair.append(attend(g, hh, n % PIPE_SLOTS))
            if hh == 1:
                bb, _, _, q0, _ = locate(g)
                o_ref[bb, 0, pl.ds(q0, NA_QB), :] = jnp.concatenate(pair, axis=0).T.astype(BF16)
                pair = []
        return carry

    lax.fori_loop(0, total_blocks // unroll, body, 0)


def _na_attention(qt, k, vt, kc, vct, bias_rows, unroll=16):
    b, d, l = qt.shape
    c = vct.shape[2] // b
    pairs = d // HEAD_PAIR
    grid_rows = l // GRID_W
    n_blocks = grid_rows // NA_Q_ROWS
    n_combos = bias_rows.shape[1]
    bps = BATCHES_PER_STEP if b % BATCHES_PER_STEP == 0 else 1
    assert grid_rows >= NA_K_ROWS and n_blocks >= 3 and n_blocks % unroll == 0
    assert (2 * unroll) % PIPE_SLOTS == 0 and PIPE_SLOTS > PIPE_AHEAD
    return pl.pallas_call(
        functools.partial(_na_kernel, n_blocks=n_blocks, grid_rows=grid_rows, unroll=unroll),
        grid=(pairs, b // bps),
        in_specs=[
            pl.BlockSpec((bps, HEAD_PAIR, l), lambda p, i: (i, p, 0)),
            pl.BlockSpec((bps, 1, l, HEAD_PAIR), lambda p, i: (i, p, 0, 0)),
            pl.BlockSpec((bps, HEAD_PAIR, l), lambda p, i: (i, p, 0)),
            pl.BlockSpec((1, 1, bps * c, HEAD_PAIR), lambda p, i: (0, p, i, 0)),
            pl.BlockSpec((1, HEAD_PAIR, bps * c), lambda p, i: (0, p, i)),
            pl.BlockSpec((2, n_combos, 2 * GRID_W), lambda p, i: (p, 0, 0)),
        ],
        out_specs=pl.BlockSpec((bps, 1, l, HEAD_PAIR), lambda p, i: (i, p, 0, 0)),
        out_shape=jax.ShapeDtypeStruct((b, pairs, l, HEAD_PAIR), BF16),
        scratch_shapes=[
            pltpu.VMEM((2, 3, NA_KB, NA_QB), F32),
            pltpu.VMEM((PIPE_SLOTS, NA_KB, NA_QB), F32),
            pltpu.VMEM((PIPE_SLOTS, c, NA_QB), F32),
            pltpu.VMEM((PIPE_SLOTS, 1, NA_QB), F32),
        ],
        compiler_params=_params("arbitrary", "arbitrary"),
        name="neighbourhood_attention",
    )(qt, k, vt, kc, vct, bias_rows)


NA_MASKED_ROW = 2 * NA_KH - 1


@functools.lru_cache(maxsize=None)
def _na_bias_plan():
    kr = np.arange(NA_K_ROWS)[:, None]
    qr = np.arange(NA_Q_ROWS)[None, :]
    top = NA_KH - 1
    variants = [
        (kr < NA_KH + 0 * qr, kr - qr + top),
        ((kr >= qr) & (kr < qr + NA_KH), kr - qr + top - NA_KH // 2),
        (kr >= NA_K_ROWS - NA_KH + 0 * qr, kr - qr + top - (NA_K_ROWS - NA_Q_ROWS)),
    ]
    idx = np.stack([np.where(valid, off, NA_MASKED_ROW) for valid, off in variants])
    assert idx.min() >= 0 and idx.max() <= NA_MASKED_ROW
    combos = []
    plan = np.zeros((3, NA_K_ROWS, NA_Q_ROWS // 2), np.int32)
    for v in range(3):
        for r in range(NA_K_ROWS):
            for half in range(NA_Q_ROWS // 2):
                key = (int(idx[v, r, 2 * half]), int(idx[v, r, 2 * half + 1]))
                if key not in combos:
                    combos.append(key)
                plan[v, r, half] = combos.index(key)
    return plan, tuple(combos)


def _na_bias_rows(rpb):
    h, n_rows, n_off = rpb.shape
    w = GRID_W
    reach = NA_KW - 1
    _, combos = _na_bias_plan()
    masked = n_rows * n_off
    lane = np.arange(2 * w)
    first = np.where(lane <= reach, reach - lane, np.where(lane >= 2 * w - reach, reach + 2 * w - lane, -1))
    second = np.where(np.abs(lane - w) <= reach, reach + w - lane, -1)
    idx = np.full((len(combos), 2 * w), masked, np.int32)
    for n, (a, b) in enumerate(combos):
        if a != NA_MASKED_ROW:
            idx[n] = np.where(first >= 0, a * n_off + first, idx[n])
        if b != NA_MASKED_ROW:
            idx[n] = np.where(second >= 0, b * n_off + second, idx[n])
    flat = jnp.concatenate([rpb.reshape(h, masked), jnp.full((h, 1), NEG_INF, rpb.dtype)], axis=1)
    return flat[:, idx]


def _ctx_kernel(q_ref, k_ref, v_ref, o_ref, *, pairs):
    def scores(p, hh):
        cols = slice(p * HEAD_PAIR, (p + 1) * HEAD_PAIR)
        return jnp.dot(k_ref[0, p], _mask_rows_to_head(q_ref[0, cols, :], hh),
                       preferred_element_type=F32)

    heads = [(p, hh) for p in range(pairs) for hh in range(2)]
    s_next = scores(*heads[0])
    outs = []
    for n, (p, hh) in enumerate(heads):
        s = s_next
        if n + 1 < len(heads):
            s_next = scores(*heads[n + 1])
        pr = jnp.exp2(s - s.max(axis=0, keepdims=True))
        l = pr.sum(axis=0, keepdims=True)
        rows = slice(p * HEAD_PAIR + hh * HEAD_DIM, p * HEAD_PAIR + (hh + 1) * HEAD_DIM)
        outs.append(jnp.dot(v_ref[0, rows, :], pr.astype(BF16), preferred_element_type=F32) / l)
        if hh == 1:
            o_ref[0, p] = jnp.concatenate(outs, axis=0).T.astype(BF16)
            outs = []


def _ctx_attention(qct, kc, vct, batches):
    _, d, bc = qct.shape
    c = bc // batches
    pairs = d // HEAD_PAIR
    return pl.pallas_call(
        functools.partial(_ctx_kernel, pairs=pairs),
        grid=(batches,),
        in_specs=[
            pl.BlockSpec((1, d, c), lambda i: (0, 0, i)),
            pl.BlockSpec((1, pairs, c, HEAD_PAIR), lambda i: (0, 0, i, 0)),
            pl.BlockSpec((1, d, c), lambda i: (0, 0, i)),
        ],
        out_specs=pl.BlockSpec((1, pairs, c, HEAD_PAIR), lambda i: (0, 0, i, 0)),
        out_shape=jax.ShapeDtypeStruct((1, pairs, bc, HEAD_PAIR), BF16),
        compiler_params=_params("arbitrary"),
        name="context_attention",
    )(qct, kc, vct)


def _swa_kernel(q_ref, k_ref, v_ref, kc_ref, vc_ref, mask_ref, sink_ref, o_ref, sl_ref, sc_ref, m_ref, *,
                n_blocks, seq, unroll):
    kv_in_pair = pl.program_id(1) % 2
    n_ctx = kc_ref.shape[2] // q_ref.shape[0]
    total_blocks = q_ref.shape[0] * n_blocks
    ones_loc = jnp.ones((BF16_SUBLANES, SWA_KB), BF16)
    ones_ctx = jnp.ones((BF16_SUBLANES, n_ctx), BF16)

    def locate(g):
        bb = g // n_blocks
        j = g - bb * n_blocks
        k0 = pl.multiple_of(jnp.clip(SWA_QB * j - SWA_WINDOW, 0, seq - SWA_KB), SWA_QB)
        return bb, j, k0, pl.multiple_of(j * SWA_QB, SWA_QB), pl.multiple_of(bb * n_ctx, n_ctx)

    hp = SWA_UNIT_HEADS
    parts = SWA_GROUP // hp

    def scores(g, part, slot):
        g = jnp.minimum(g, total_blocks - 1)
        bb, j, k0, q0, c0 = locate(g)
        variant = jnp.where(j == 0, 0, jnp.where(j == n_blocks - 1, 2, 1))
        heads = range(part * hp, (part + 1) * hp)
        qg = jnp.concatenate(
            [q_ref[bb, h * HEAD_DIM:(h + 1) * HEAD_DIM, pl.ds(q0, SWA_QB)] for h in heads], axis=1)
        qm = _mask_rows_to_head(jnp.concatenate([qg, qg], axis=0), kv_in_pair)
        mask = mask_ref[variant]
        s_loc = (jnp.dot(k_ref[bb, 0, pl.ds(k0, SWA_KB), :], qm, preferred_element_type=F32)
                 + jnp.concatenate([mask] * hp, axis=1))
        s_ctx = jnp.dot(kc_ref[0, 0, pl.ds(c0, n_ctx), :], qm, preferred_element_type=F32)
        sl_ref[slot] = s_loc
        sc_ref[slot] = s_ctx
        m_ref[slot] = jnp.maximum(_col_max(s_loc, s_ctx), sink_ref[0, :, part * hp * SWA_QB:(part + 1) * hp * SWA_QB])

    def attend(g, part, slot):
        bb, _, k0, q0, c0 = locate(g)
        m = m_ref[slot]
        sink = sink_ref[0, :, part * hp * SWA_QB:(part + 1) * hp * SWA_QB]
        p_loc = jnp.exp2(sl_ref[slot] - m).astype(BF16)
        p_ctx = jnp.exp2(sc_ref[slot] - m).astype(BF16)
        v_loc = jnp.concatenate([v_ref[bb, :, pl.ds(k0, SWA_KB)], ones_loc], axis=0)
        v_ctx = jnp.concatenate([vc_ref[0, :, pl.ds(c0, n_ctx)], ones_ctx], axis=0)
        o = (jnp.dot(v_loc, p_loc, preferred_element_type=F32)
             + jnp.dot(v_ctx, p_ctx, preferred_element_type=F32))
        o = o[:HEAD_DIM] / (o[HEAD_DIM:HEAD_DIM + 1] + jnp.exp2(sink - m))
        o_rows = jnp.concatenate([o[:, h * SWA_QB:(h + 1) * SWA_QB] for h in range(hp)], axis=0)
        o_ref[bb, part, pl.ds(q0, SWA_QB), :] = o_rows.T.astype(BF16)

    for n in range(SWA_PIPE_AHEAD):
        scores(n // parts, n % parts, n)

    def body(it, carry):
        base = it * unroll
        for n in range(unroll * parts):
            ahead = n + SWA_PIPE_AHEAD
            scores(base + ahead // parts, ahead % parts, ahead % PIPE_SLOTS)
            attend(base + n // parts, n % parts, n % PIPE_SLOTS)
        return carry

    lax.fori_loop(0, total_blocks // unroll, body, 0)


def _swa_mask_tables():
    kr = np.arange(SWA_KB)[:, None]
    qc = np.arange(SWA_QB)[None, :]
    rel = [kr - qc, kr - SWA_WINDOW - qc, kr - (SWA_KB - SWA_QB) - qc]
    return np.stack([np.where(np.abs(r) <= SWA_WINDOW, 0.0, NEG_INF) for r in rel]).astype(np.float32)


def _swa_attention(qt, k, vt, kc, vct, sink_rows, unroll=32):
    b, d, l = qt.shape
    c = vct.shape[2] // b
    kv = vt.shape[1] // HEAD_DIM
    gw = SWA_GROUP * HEAD_DIM
    parts = SWA_GROUP // SWA_UNIT_HEADS
    n_blocks = l // SWA_QB
    bps = BATCHES_PER_STEP if b % BATCHES_PER_STEP == 0 else 1
    assert SWA_UNIT_HEADS * HEAD_DIM == HEAD_PAIR
    assert l >= SWA_KB and n_blocks >= 3 and n_blocks % unroll == 0
    assert (unroll * SWA_GROUP // SWA_UNIT_HEADS) % PIPE_SLOTS == 0 and PIPE_SLOTS > SWA_PIPE_AHEAD
    return pl.pallas_call(
        functools.partial(_swa_kernel, n_blocks=n_blocks, seq=l, unroll=unroll),
        grid=(b // bps, kv),
        in_specs=[
            pl.BlockSpec((bps, gw, l), lambda i, n: (i, n, 0)),
            pl.BlockSpec((bps, 1, l, HEAD_PAIR), lambda i, n: (i, n // 2, 0, 0)),
            pl.BlockSpec((bps, HEAD_DIM, l), lambda i, n: (i, n, 0)),
            pl.BlockSpec((1, 1, bps * c, HEAD_PAIR), lambda i, n: (0, n // 2, i, 0)),
            pl.BlockSpec((1, HEAD_DIM, bps * c), lambda i, n: (0, n, i)),
            _const_spec((3, SWA_KB, SWA_QB)),
            pl.BlockSpec((1, 1, SWA_GROUP * SWA_QB), lambda i, n: (n, 0, 0)),
        ],
        out_specs=pl.BlockSpec((bps, parts, l, HEAD_PAIR), lambda i, n: (i, n, 0, 0)),
        out_shape=jax.ShapeDtypeStruct((b, d // HEAD_PAIR, l, HEAD_PAIR), BF16),
        scratch_shapes=[
            pltpu.VMEM((PIPE_SLOTS, SWA_KB, SWA_UNIT_HEADS * SWA_QB), F32),
            pltpu.VMEM((PIPE_SLOTS, c, SWA_UNIT_HEADS * SWA_QB), F32),
            pltpu.VMEM((PIPE_SLOTS, 1, SWA_UNIT_HEADS * SWA_QB), F32),
        ],
        compiler_params=_params("arbitrary", "arbitrary"),
        name="sliding_window_attention",
    )(qt, k, vt, kc, vct, jnp.asarray(_swa_mask_tables()), sink_rows)


def _post_kernel(a_ref, h_ref, mod_ref, g_ref, wo_ref, w1_ref, w2_ref, o_ref, *, ff_chunk):
    mod = mod_ref[0]
    a = jnp.concatenate([a_ref[0, p] for p in range(a_ref.shape[1])], axis=1)
    y = jnp.dot(a, wo_ref[...], preferred_element_type=F32)
    h1 = h_ref[0] + mod[2:3] * y
    u = _rms_modulate(h1, g_ref[...], mod[3:4], mod[4:5]).astype(BF16)
    d_ff = w1_ref.shape[1]
    acc = jnp.zeros_like(h1)
    for c0 in range(0, d_ff, ff_chunk):
        t = jnp.dot(u, w1_ref[:, c0:c0 + ff_chunk], preferred_element_type=F32)
        r = jnp.square(jnp.maximum(t, 0.0)).astype(BF16)
        acc = acc + jnp.dot(r, w2_ref[c0:c0 + ff_chunk, :], preferred_element_type=F32)
    o_ref[0] = h1 + mod[5:6] * acc


def _post_block(a, h, mod, g, wo, w1, w2, *, tm, ff_chunk=1024):
    b, t, d = h.shape
    return pl.pallas_call(
        functools.partial(_post_kernel, ff_chunk=ff_chunk),
        grid=(b, t // tm),
        in_specs=[
            pl.BlockSpec((1, d // HEAD_PAIR, tm, HEAD_PAIR), lambda i, j: (i, 0, j, 0)),
            pl.BlockSpec((1, tm, d), lambda i, j: (i, j, 0)),
            pl.BlockSpec((1, 6, d), lambda i, j: (i, 0, 0)),
            _const_spec((1, d)),
            _const_spec(wo.shape),
            _const_spec(w1.shape),
            _const_spec(w2.shape),
        ],
        out_specs=pl.BlockSpec((1, tm, d), lambda i, j: (i, j, 0)),
        out_shape=jax.ShapeDtypeStruct((b, t, d), F32),
        compiler_params=_params("arbitrary", "arbitrary"),
        name="outproj_mlp",
    )(a, h, mod, g, wo, w1, w2)


def _rope_tables_t(l):
    t = np.arange(l)
    n_freq = HEAD_DIM // 4
    inv = ROPE_BASE ** (-np.arange(n_freq, dtype=np.float64) / n_freq)
    ang = np.stack([(t // GRID_W)[:, None] * inv, (t % GRID_W)[:, None] * inv], axis=1)
    cos = np.transpose(np.cos(ang), (1, 2, 0))
    sin = np.transpose(np.sin(ang), (1, 2, 0))
    cos_t = np.stack([cos, cos], axis=1).reshape(HEAD_DIM, l)
    sin_t = np.stack([-sin, sin], axis=1).reshape(HEAD_DIM, l)
    return jnp.asarray(cos_t, F32), jnp.asarray(sin_t, F32)


def _swap_rotary_halves(v):
    q = HEAD_DIM // 4
    return jnp.concatenate([v[q:2 * q], v[0:q], v[3 * q:4 * q], v[2 * q:3 * q]], axis=0)


def _gain_rope_tables(gain, cos_t, sin_t):
    gain = gain.astype(F32)
    return ((gain[:, None] * cos_t).astype(BF16), (_swap_rotary_halves(gain)[:, None] * sin_t).astype(BF16))


def _gain_cols(gain, scale=1.0):
    return jnp.broadcast_to((gain.astype(F32) * scale)[:, None], (HEAD_DIM, QKV_SUB))


def kernel(x, c, ctx, c_ctx, ada_w, ada_b, g_mix, g_mlp, mlp_w1, mlp_w2, na_wqkv, na_q_gain, na_k_gain,
           na_rpb, na_wo, swa_wqkv, swa_q_gain, swa_k_gain, swa_sink, swa_wo):
    b, l, d = x.shape
    n_ctx = ctx.shape[1]
    tm = TM_POST
    tmc = b * n_ctx
    ctx_flat = ctx.reshape(1, tmc, d)
    scale = HEAD_DIM ** -0.5 * LOG2E

    cond =jnp.concatenate([c, c_ctx[None], jnp.zeros((8 - b - 1, d), F32)], axis=0)
    mods = _modulation(cond, ada_w, ada_b)

    def mods_of(i):
        lat = mods[i, :b].reshape(b, 6, d)
        cx = mods[i, b].reshape(1, 6, d)
        return lat, cx

    def row(v):
        return v.reshape(1, d)

    mod_lat, mod_ctx = mods_of(0)
    gq, gk = na_q_gain[0], na_k_gain[0]
    gains = (_gain_cols(gq, scale), _gain_cols(gk))
    qt, k, vt, wt = _qkv_project(x, mod_lat, row(g_mix[0]), na_wqkv[0], gains, None, kv_width=d, tm=TM_QKV)
    qct, kc, vct = _qkv_project(ctx_flat, mod_ctx, row(g_mix[0]), wt, gains, None, kv_width=d, tm=tmc)
    a_lat = _na_attention(qt, k, vt, kc, vct, _na_bias_rows(na_rpb[0] * LOG2E))
    a_ctx = _ctx_attention(qct, kc, vct, b)
    wo = na_wo[0].astype(BF16)
    w1 = mlp_w1[0].astype(BF16)
    w2 = mlp_w2[0].astype(BF16)
    h_lat = _post_block(a_lat, x, mod_lat, row(g_mlp[0]), wo, w1, w2, tm=tm)
    h_ctx = _post_block(a_ctx, ctx_flat, mod_ctx, row(g_mlp[0]), wo, w1, w2, tm=tmc)

    mod_lat, mod_ctx = mods_of(1)
    kvw = SWA_KV_HEADS * HEAD_DIM
    gq, gk = swa_q_gain[0], swa_k_gain[0]
    cos_t, sin_t = _rope_tables_t(l)
    rope_tabs = (_gain_rope_tables(gq * scale, cos_t, sin_t) + _gain_rope_tables(gk, cos_t, sin_t))
    qt, k, vt, wt = _qkv_project(h_lat, mod_lat, row(g_mix[1]), swa_wqkv[0], None, rope_tabs, kv_width=kvw,
                                 tm=TM_QKV)
    gains = (_gain_cols(gq, scale), _gain_cols(gk))
    _, kc, vct = _qkv_project(h_ctx, mod_ctx, row(g_mix[1]), wt, gains, None, kv_width=kvw, tm=tmc)
    sink_rows = jnp.repeat(swa_sink[0].astype(F32) * LOG2E, SWA_QB).reshape(SWA_KV_HEADS, 1, SWA_GROUP * SWA_QB)
    a_lat = _swa_attention(qt, k, vt, kc, vct, sink_rows)
    return _post_block(a_lat, h_lat, mod_lat, row(g_mlp[1]), swa_wo[0].astype(BF16),
                       mlp_w1[1].astype(BF16), mlp_w2[1].astype(BF16), tm=tm)
```

```python
import functools

import numpy as np
import jax
import jax.numpy as jnp
from jax import lax
from jax.experimental import pallas as pl
from jax.experimental.pallas import tpu as pltpu

GRID_W = 64
N_HEADS = 16
HEAD_DIM = 64
NA_KH = 8
NA_KW = 16
SWA_KV_HEADS = 4
SWA_GROUP = N_HEADS // SWA_KV_HEADS
SWA_WINDOW = 128
ROPE_BASE = 10000.0
NORM_EPS = 1e-6
NEG_INF = -1e30
LOG2E = float(np.log2(np.e))

F32 = jnp.float32
BF16 = jnp.bfloat16

V7X_VMEM_BYTES = 64 * 1024 * 1024
VMEM_LIMIT_BYTES = V7X_VMEM_BYTES - 8 * 1024 * 1024

BF16_SUBLANES = 16
HEAD_PAIR = 2 * HEAD_DIM
NA_Q_ROWS = 4
NA_K_ROWS = NA_Q_ROWS + NA_KH
NA_QB = NA_Q_ROWS * GRID_W
NA_KB = NA_K_ROWS * GRID_W
SWA_QB = 128
SWA_KB = SWA_QB + 2 * SWA_WINDOW
SWA_UNIT_HEADS = 2
TM_QKV = 1024
TM_POST = 1024
QKV_SUB = 256
QKV_CHUNK = 256
QKV_CHUNK_ROPE = 1024
PIPE_AHEAD = 2
SWA_PIPE_AHEAD = 3
PIPE_SLOTS = 4


def _params(*semantics):
    return pltpu.CompilerParams(dimension_semantics=semantics, vmem_limit_bytes=VMEM_LIMIT_BYTES)


def _const_spec(shape):
    zeros = (0,) * len(shape)
    return pl.BlockSpec(shape, lambda *_: zeros, pipeline_mode=pl.Buffered(1))


def _mod_kernel(cond_ref, w_ref, b_ref, o_ref):
    s = jax.nn.silu(cond_ref[...]).astype(BF16)
    o_ref[0] = jnp.dot(s, w_ref[0].astype(BF16), preferred_element_type=F32) + b_ref[0]


def _modulation(cond, ada_w, ada_b, tn=1536):
    depth, d, n = ada_w.shape
    rows = cond.shape[0]
    return pl.pallas_call(
        _mod_kernel,
        grid=(depth, n // tn),
        in_specs=[
            pl.BlockSpec((rows, d), lambda i, j: (0, 0)),
            pl.BlockSpec((1, d, tn), lambda i, j: (i, 0, j)),
            pl.BlockSpec((1, 1, tn), lambda i, j: (i, 0, j)),
        ],
        out_specs=pl.BlockSpec((1, rows, tn), lambda i, j: (i, 0, j)),
        out_shape=jax.ShapeDtypeStruct((depth, rows, n), F32),
        compiler_params=_params("arbitrary", "arbitrary"),
        name="adaln_modulation",
    )(cond, ada_w, ada_b.reshape(depth, 1, n))


def _rms_modulate(x, g, shift, scale):
    y = x * lax.rsqrt(jnp.mean(x * x, axis=-1, keepdims=True) + NORM_EPS) * g
    return y * (1.0 + scale) + shift


def _head_norm_t(a, gain_b):
    n = a.shape[0] // HEAD_DIM
    tm = a.shape[1]
    a = a.reshape(n, HEAD_DIM, tm)
    a = (a * lax.rsqrt(jnp.mean(a * a, axis=1, keepdims=True) + NORM_EPS)).astype(BF16) * gain_b[None]
    return a.reshape(n * HEAD_DIM, tm)


def _head_norm_rope_t(a, gain_cos, gain_sin):
    n = a.shape[0] // HEAD_DIM
    tm = a.shape[1]
    a = a.reshape(n, HEAD_DIM, tm)
    a = (a * lax.rsqrt(jnp.mean(a * a, axis=1, keepdims=True) + NORM_EPS)).astype(BF16)
    q = HEAD_DIM // 4
    swapped = jnp.concatenate([a[:, q:2 * q], a[:, 0:q], a[:, 3 * q:4 * q], a[:, 2 * q:3 * q]], axis=1)
    a = a * gain_cos[None] + swapped * gain_sin[None]
    return a.reshape(n * HEAD_DIM, tm)


def _qkv_kernel(*refs, d_model, kv_width, use_rope, chunk, sub, prepare_weight):
    if prepare_weight:
        *refs, wt_ref = refs
    if use_rope:
        h_ref, mod_ref, g_ref, w_ref, qc_ref, qs_ref, kc_ref, ks_ref, q_ref, k_ref, v_ref = refs
    else:
        h_ref, mod_ref, g_ref, w_ref, gq_ref, gk_ref, q_ref, k_ref, v_ref = refs
        gq = gq_ref[...].astype(BF16)
        gk = gk_ref[...].astype(BF16)
    if prepare_weight:
        @pl.when((pl.program_id(0) == 0) & (pl.program_id(1) == 0))
        def _():
            for c0 in range(0, wt_ref.shape[0], QKV_SUB):
                wt_ref[c0:c0 + QKV_SUB, :] = w_ref[:, c0:c0 + QKV_SUB].T.astype(BF16)
    else:
        wt_ref = w_ref
    mod = mod_ref[0]
    g_eff = (g_ref[...] * (1.0 + mod[1:2])).astype(BF16)
    shift = mod[0:1].astype(BF16)
    contract_last = (((1,), (1,)), ((), ()))
    kv_chunk = min(chunk, kv_width)
    for t0 in range(0, h_ref.shape[1], sub):
        toks = slice(t0, t0 + sub)
        x = h_ref[0, toks, :]
        xn = (x * lax.rsqrt(jnp.mean(x * x, axis=-1, keepdims=True) + NORM_EPS)).astype(BF16)
        u = xn * g_eff + shift

        def proj(r0, rows):
            return lax.dot_general(wt_ref[r0:r0 + rows, :], u, contract_last,
                                   preferred_element_type=F32)

        def norm_q(a):
            if use_rope:
                return _head_norm_rope_t(a, qc_ref[:, toks], qs_ref[:, toks])
            return _head_norm_t(a, gq)

        def norm_k(a):
            if use_rope:
                return _head_norm_rope_t(a, kc_ref[:, toks], ks_ref[:, toks])
            return _head_norm_t(a, gk)

        for r0 in range(0, d_model, chunk):
            q_ref[0, r0:r0 + chunk, toks] = norm_q(proj(r0, chunk))
        for r0 in range(0, kv_width, kv_chunk):
            kt = norm_k(proj(d_model + r0, kv_chunk)).T
            for c0 in range(0, kv_chunk, HEAD_PAIR):
                k_ref[0, (r0 + c0) // HEAD_PAIR, toks, :] = kt[:, c0:c0 + HEAD_PAIR]
        for r0 in range(0, kv_width, kv_chunk):
            v_ref[0, r0:r0 + kv_chunk, toks] = proj(d_model + kv_width + r0, kv_chunk).astype(BF16)


def _qkv_project(h, mod, g, w, gains, rope_tabs, *, kv_width, tm, sub=QKV_SUB):
    b, t, d = h.shape
    use_rope = rope_tabs is not None
    prepare_weight = w.dtype == F32
    n_out = d + 2 * kv_width
    assert w.shape == ((d, n_out) if prepare_weight else (n_out, d))
    chunk = QKV_CHUNK_ROPE if use_rope else QKV_CHUNK
    in_specs = [
        pl.BlockSpec((1, tm, d), lambda i, j: (i, j, 0)),
        pl.BlockSpec((1, 6, d), lambda i, j: (i, 0, 0)),
        _const_spec((1, d)),
        _const_spec(w.shape),
    ]
    args = [h, mod, g, w]
    if use_rope:
        in_specs += [pl.BlockSpec((HEAD_DIM, tm), lambda i, j: (0, j))] * 4
        args += list(rope_tabs)
    else:
        in_specs += [_const_spec((HEAD_DIM, sub))] * 2
        args += list(gains)
    out_specs = [
        pl.BlockSpec((1, d, tm), lambda i, j: (i, 0, j)),
        pl.BlockSpec((1, kv_width // HEAD_PAIR, tm, HEAD_PAIR), lambda i, j: (i, 0, j, 0)),
        pl.BlockSpec((1, kv_width, tm), lambda i, j: (i, 0, j)),
    ]
    out_shape = [
        jax.ShapeDtypeStruct((b, d, t), BF16),
        jax.ShapeDtypeStruct((b, kv_width // HEAD_PAIR, t, HEAD_PAIR), BF16),
        jax.ShapeDtypeStruct((b, kv_width, t), BF16),
    ]
    if prepare_weight:
        out_specs.append(_const_spec((n_out, d)))
        out_shape.append(jax.ShapeDtypeStruct((n_out, d), BF16))
    return pl.pallas_call(
        functools.partial(_qkv_kernel, d_model=d, kv_width=kv_width, use_rope=use_rope, chunk=chunk,
                          sub=sub, prepare_weight=prepare_weight),
        grid=(b, t // tm),
        in_specs=in_specs,
        out_specs=out_specs,
        out_shape=out_shape,
        compiler_params=_params("arbitrary", "arbitrary"),
        name="qkv_rope" if use_rope else "qkv",
    )(*args)


def _col_max(*parts):
    m = parts[0].max(axis=0, keepdims=True)
    for p in parts[1:]:
        m = jnp.maximum(m, p.max(axis=0, keepdims=True))
    return m


def _mask_rows_to_head(q_pair, head):
    row = lax.broadcasted_iota(jnp.int32, q_pair.shape, 0)
    return jnp.where((row // HEAD_DIM) == head, q_pair, jnp.zeros_like(q_pair))


def _na_build_bias(rows_ref, bias_ref):
    w = GRID_W
    kc = lax.broadcasted_iota(jnp.int32, (w, 2 * w), 0)
    qc = lax.broadcasted_iota(jnp.int32, (w, 2 * w), 1) % w
    c0 = jnp.clip(qc - NA_KW // 2, 0, w - NA_KW)
    col_valid = (kc >= c0) & (kc < c0 + NA_KW)
    plan, combos = _na_bias_plan()
    for hh in range(2):
        tiles = []
        for n in range(len(combos)):
            row = jnp.broadcast_to(rows_ref[hh, n:n + 1, :], (w, 2 * w))
            tiles.append(jnp.where(col_valid, pltpu.roll(row, 0, 1, stride=1, stride_axis=0), NEG_INF))
        for v in range(3):
            for kr in range(NA_K_ROWS):
                for half in range(NA_Q_ROWS // 2):
                    bias_ref[hh, v, kr * w:(kr + 1) * w, half * 2 * w:(half + 1) * 2 * w] = tiles[plan[v, kr, half]]


def _na_kernel(q_ref, k_ref, v_ref, kc_ref, vc_ref, rows_ref, o_ref, bias_ref, sl_ref, sc_ref, m_ref, *,
               n_blocks, grid_rows, unroll):
    @pl.when(pl.program_id(1) == 0)
    def _():
        _na_build_bias(rows_ref, bias_ref)

    ones_loc = jnp.ones((BF16_SUBLANES, NA_KB), BF16)
    ones_ctx = jnp.ones((BF16_SUBLANES, kc_ref.shape[2]), BF16)

    def window(i):
        start_row = jnp.clip(NA_Q_ROWS * i - NA_KH // 2, 0, grid_rows - NA_K_ROWS)
        return pl.multiple_of(start_row * GRID_W, NA_QB), pl.multiple_of(i * NA_QB, NA_QB)

    def scores(i, hh, slot):
        i = jnp.minimum(i, n_blocks - 1)
        k0, q0 = window(i)
        variant = jnp.where(i == 0, 0, jnp.where(i == n_blocks - 1, 2, 1))
        qm = _mask_rows_to_head(q_ref[0, :, pl.ds(q0, NA_QB)], hh)
        s_loc = (jnp.dot(k_ref[0, 0, pl.ds(k0, NA_KB), :], qm, preferred_element_type=F32)
                 + bias_ref[hh, variant])
        s_ctx = jnp.dot(kc_ref[0, 0], qm, preferred_element_type=F32)
        sl_ref[slot] = s_loc
        sc_ref[slot] = s_ctx
        m_ref[slot] = _col_max(s_loc, s_ctx)

    def attend(i, hh, slot):
        k0, _ = window(i)
        m = m_ref[slot]
        p_loc = jnp.exp2(sl_ref[slot] - m).astype(BF16)
        p_ctx = jnp.exp2(sc_ref[slot] - m).astype(BF16)
        rows = slice(hh * HEAD_DIM, (hh + 1) * HEAD_DIM)
        v_loc = jnp.concatenate([v_ref[0, rows, pl.ds(k0, NA_KB)], ones_loc], axis=0)
        v_ctx = jnp.concatenate([vc_ref[0, rows, :], ones_ctx], axis=0)
        o = (jnp.dot(v_loc, p_loc, preferred_element_type=F32)
             + jnp.dot(v_ctx, p_ctx, preferred_element_type=F32))
        return o[:HEAD_DIM] / o[HEAD_DIM:HEAD_DIM + 1]

    for n in range(PIPE_AHEAD):
        scores(n // 2, n % 2, n)

    single_trip = n_blocks == unroll

    def body(it, carry):
        base = it * unroll
        units = [(base + blk, hh) for blk in range(unroll + (PIPE_AHEAD + 1) // 2) for hh in range(2)]
        pair = []
        for n, (i, hh) in enumerate(units[:2 * unroll]):
            ahead = units[n + PIPE_AHEAD]
            if not (single_trip and ahead[0] >= n_blocks):
                scores(ahead[0], ahead[1], (n + PIPE_AHEAD) % PIPE_SLOTS)
            pair.append(attend(i, hh, n % PIPE_SLOTS))
            if hh == 1:
                q0 = pl.multiple_of(i * NA_QB, NA_QB)
                o_ref[0, 0, pl.ds(q0, NA_QB), :] = jnp.concatenate(pair, axis=0).T.astype(BF16)
                pair = []
        return carry

    if single_trip:
        body(0, 0)
    else:
        lax.fori_loop(0, n_blocks // unroll, body, 0)


def _na_attention(qt, k, vt, kc, vct, bias_rows, unroll=16):
    b, d, l = qt.shape
    c = vct.shape[2] // b
    pairs = d // HEAD_PAIR
    grid_rows = l // GRID_W
    n_blocks = grid_rows // NA_Q_ROWS
    n_combos = bias_rows.shape[1]
    assert grid_rows >= NA_K_ROWS and n_blocks >= 3 and n_blocks % unroll == 0
    assert (2 * unroll) % PIPE_SLOTS == 0 and PIPE_SLOTS > PIPE_AHEAD
    return pl.pallas_call(
        functools.partial(_na_kernel, n_blocks=n_blocks, grid_rows=grid_rows, unroll=unroll),
        grid=(pairs, b),
        in_specs=[
            pl.BlockSpec((1, HEAD_PAIR, l), lambda p, i: (i, p, 0)),
            pl.BlockSpec((1, 1, l, HEAD_PAIR), lambda p, i: (i, p, 0, 0)),
            pl.BlockSpec((1, HEAD_PAIR, l), lambda p, i: (i, p, 0)),
            pl.BlockSpec((1, 1, c, HEAD_PAIR), lambda p, i: (0, p, i, 0)),
            pl.BlockSpec((1, HEAD_PAIR, c), lambda p, i: (0, p, i)),
            pl.BlockSpec((2, n_combos, 2 * GRID_W), lambda p, i: (p, 0, 0)),
        ],
        out_specs=pl.BlockSpec((1, 1, l, HEAD_PAIR), lambda p, i: (i, p, 0, 0)),
        out_shape=jax.ShapeDtypeStruct((b, pairs, l, HEAD_PAIR), BF16),
        scratch_shapes=[
            pltpu.VMEM((2, 3, NA_KB, NA_QB), F32),
            pltpu.VMEM((PIPE_SLOTS, NA_KB, NA_QB), F32),
            pltpu.VMEM((PIPE_SLOTS, c, NA_QB), F32),
            pltpu.VMEM((PIPE_SLOTS, 1, NA_QB), F32),
        ],
        compiler_params=_params("arbitrary", "arbitrary"),
        name="neighbourhood_attention",
    )(qt, k, vt, kc, vct, bias_rows)


NA_MASKED_ROW = 2 * NA_KH - 1


@functools.lru_cache(maxsize=None)
def _na_bias_plan():
    kr = np.arange(NA_K_ROWS)[:, None]
    qr = np.arange(NA_Q_ROWS)[None, :]
    top = NA_KH - 1
    variants = [
        (kr < NA_KH + 0 * qr, kr - qr + top),
        ((kr >= qr) & (kr < qr + NA_KH), kr - qr + top - NA_KH // 2),
        (kr >= NA_K_ROWS - NA_KH + 0 * qr, kr - qr + top - (NA_K_ROWS - NA_Q_ROWS)),
    ]
    idx = np.stack([np.where(valid, off, NA_MASKED_ROW) for valid, off in variants])
    assert idx.min() >= 0 and idx.max() <= NA_MASKED_ROW
    combos = []
    plan = np.zeros((3, NA_K_ROWS, NA_Q_ROWS // 2), np.int32)
    for v in range(3):
        for r in range(NA_K_ROWS):
            for half in range(NA_Q_ROWS // 2):
                key = (int(idx[v, r, 2 * half]), int(idx[v, r, 2 * half + 1]))
                if key not in combos:
                    combos.append(key)
                plan[v, r, half] = combos.index(key)
    return plan, tuple(combos)


def _na_bias_rows(rpb):
    h, n_rows, n_off = rpb.shape
    w = GRID_W
    reach = NA_KW - 1
    _, combos = _na_bias_plan()
    masked = n_rows * n_off
    lane = np.arange(2 * w)
    first = np.where(lane <= reach, reach - lane, np.where(lane >= 2 * w - reach, reach + 2 * w - lane, -1))
    second = np.where(np.abs(lane - w) <= reach, reach + w - lane, -1)
    idx = np.full((len(combos), 2 * w), masked, np.int32)
    for n, (a, b) in enumerate(combos):
        if a != NA_MASKED_ROW:
            idx[n] = np.where(first >= 0, a * n_off + first, idx[n])
        if b != NA_MASKED_ROW:
            idx[n] = np.where(second >= 0, b * n_off + second, idx[n])
    flat = jnp.concatenate([rpb.reshape(h, masked), jnp.full((h, 1), NEG_INF, rpb.dtype)], axis=1)
    return flat[:, idx]


def _ctx_kernel(q_ref, k_ref, v_ref, o_ref, *, pairs):
    def scores(p, hh):
        cols = slice(p * HEAD_PAIR, (p + 1) * HEAD_PAIR)
        return jnp.dot(k_ref[0, p], _mask_rows_to_head(q_ref[0, cols, :], hh),
                       preferred_element_type=F32)

    heads = [(p, hh) for p in range(pairs) for hh in range(2)]
    s_next = scores(*heads[0])
    outs = []
    for n, (p, hh) in enumerate(heads):
        s = s_next
        if n + 1 < len(heads):
            s_next = scores(*heads[n + 1])
        pr = jnp.exp2(s - s.max(axis=0, keepdims=True))
        l = pr.sum(axis=0, keepdims=True)
        rows = slice(p * HEAD_PAIR + hh * HEAD_DIM, p * HEAD_PAIR + (hh + 1) * HEAD_DIM)
        outs.append(jnp.dot(v_ref[0, rows, :], pr.astype(BF16), preferred_element_type=F32) / l)
        if hh == 1:
            o_ref[0, p] = jnp.concatenate(outs, axis=0).T.astype(BF16)
            outs = []


def _ctx_attention(qct, kc, vct, batches):
    _, d, bc = qct.shape
    c = bc // batches
    pairs = d // HEAD_PAIR
    return pl.pallas_call(
        functools.partial(_ctx_kernel, pairs=pairs),
        grid=(batches,),
        in_specs=[
            pl.BlockSpec((1, d, c), lambda i: (0, 0, i)),
            pl.BlockSpec((1, pairs, c, HEAD_PAIR), lambda i: (0, 0, i, 0)),
            pl.BlockSpec((1, d, c), lambda i: (0, 0, i)),
        ],
        out_specs=pl.BlockSpec((1, pairs, c, HEAD_PAIR), lambda i: (0, 0, i, 0)),
        out_shape=jax.ShapeDtypeStruct((1, pairs, bc, HEAD_PAIR), BF16),
        compiler_params=_params("arbitrary"),
        name="context_attention",
    )(qct, kc, vct)


def _swa_kernel(q_ref, k_ref, v_ref, kc_ref, vc_ref, mask_ref, sink_ref, o_ref, sl_ref, sc_ref, m_ref, *,
                n_blocks, seq, unroll):
    kv_in_pair = pl.program_id(1) % 2
    ones_loc = jnp.ones((BF16_SUBLANES, SWA_KB), BF16)
    ones_ctx = jnp.ones((BF16_SUBLANES, kc_ref.shape[2]), BF16)

    def window(j):
        k0 = pl.multiple_of(jnp.clip(SWA_QB * j - SWA_WINDOW, 0, seq - SWA_KB), SWA_QB)
        return k0, pl.multiple_of(j * SWA_QB, SWA_QB)

    hp = SWA_UNIT_HEADS
    parts = SWA_GROUP // hp

    def scores(j, part, slot):
        j = jnp.minimum(j, n_blocks - 1)
        k0, q0 = window(j)
        variant = jnp.where(j == 0, 0, jnp.where(j == n_blocks - 1, 2, 1))
        heads = range(part * hp, (part + 1) * hp)
        qg = jnp.concatenate(
            [q_ref[0, g * HEAD_DIM:(g + 1) * HEAD_DIM, pl.ds(q0, SWA_QB)] for g in heads], axis=1)
        qm = _mask_rows_to_head(jnp.concatenate([qg, qg], axis=0), kv_in_pair)
        mask = mask_ref[variant]
        s_loc = (jnp.dot(k_ref[0, 0, pl.ds(k0, SWA_KB), :], qm, preferred_element_type=F32)
                 + jnp.concatenate([mask] * hp, axis=1))
        s_ctx = jnp.dot(kc_ref[0, 0], qm, preferred_element_type=F32)
        sl_ref[slot] = s_loc
        sc_ref[slot] = s_ctx
        m_ref[slot] = jnp.maximum(_col_max(s_loc, s_ctx), sink_ref[0, :, part * hp * SWA_QB:(part + 1) * hp * SWA_QB])

    def attend(j, part, slot):
        k0, q0 = window(j)
        m = m_ref[slot]
        sink = sink_ref[0, :, part * hp * SWA_QB:(part + 1) * hp * SWA_QB]
        p_loc = jnp.exp2(sl_ref[slot] - m).astype(BF16)
        p_ctx = jnp.exp2(sc_ref[slot] - m).astype(BF16)
        v_loc = jnp.concatenate([v_ref[0, :, pl.ds(k0, SWA_KB)], ones_loc], axis=0)
        v_ctx = jnp.concatenate([vc_ref[0], ones_ctx], axis=0)
        o = (jnp.dot(v_loc, p_loc, preferred_element_type=F32)
             + jnp.dot(v_ctx, p_ctx, preferred_element_type=F32))
        o = o[:HEAD_DIM] / (o[HEAD_DIM:HEAD_DIM + 1] + jnp.exp2(sink - m))
        o_rows = jnp.concatenate([o[:, g * SWA_QB:(g + 1) * SWA_QB] for g in range(hp)], axis=0)
        o_ref[0, part, pl.ds(q0, SWA_QB), :] = o_rows.T.astype(BF16)

    for n in range(SWA_PIPE_AHEAD):
        scores(n // parts, n % parts, n)

    def body(it, carry):
        base = it * unroll
        for n in range(unroll * parts):
            ahead = n + SWA_PIPE_AHEAD
            scores(base + ahead // parts, ahead % parts, ahead % PIPE_SLOTS)
            attend(base + n // parts, n % parts, n % PIPE_SLOTS)
        return carry

    lax.fori_loop(0, n_blocks // unroll, body, 0)


def _swa_mask_tables():
    kr = np.arange(SWA_KB)[:, None]
    qc = np.arange(SWA_QB)[None, :]
    rel = [kr - qc, kr - SWA_WINDOW - qc, kr - (SWA_KB - SWA_QB) - qc]
    return np.stack([np.where(np.abs(r) <= SWA_WINDOW, 0.0, NEG_INF) for r in rel]).astype(np.float32)


def _swa_attention(qt, k, vt, kc, vct, sink_rows, unroll=32):
    b, d, l = qt.shape
    c = vct.shape[2] // b
    kv = vt.shape[1] // HEAD_DIM
    gw = SWA_GROUP * HEAD_DIM
    parts = SWA_GROUP // SWA_UNIT_HEADS
    n_blocks = l // SWA_QB
    assert SWA_UNIT_HEADS * HEAD_DIM == HEAD_PAIR
    assert l >= SWA_KB and n_blocks >= 3 and n_blocks % unroll == 0
    assert (unroll * SWA_GROUP // SWA_UNIT_HEADS) % PIPE_SLOTS == 0 and PIPE_SLOTS > SWA_PIPE_AHEAD
    return pl.pallas_call(
        functools.partial(_swa_kernel, n_blocks=n_blocks, seq=l, unroll=unroll),
        grid=(b, kv),
        in_specs=[
            pl.BlockSpec((1, gw, l), lambda i, n: (i, n, 0)),
            pl.BlockSpec((1, 1, l, HEAD_PAIR), lambda i, n: (i, n // 2, 0, 0)),
            pl.BlockSpec((1, HEAD_DIM, l), lambda i, n: (i, n, 0)),
            pl.BlockSpec((1, 1, c, HEAD_PAIR), lambda i, n: (0, n // 2, i, 0)),
            pl.BlockSpec((1, HEAD_DIM, c), lambda i, n: (0, n, i)),
            _const_spec((3, SWA_KB, SWA_QB)),
            pl.BlockSpec((1, 1, SWA_GROUP * SWA_QB), lambda i, n: (n, 0, 0)),
        ],
        out_specs=pl.BlockSpec((1, parts, l, HEAD_PAIR), lambda i, n: (i, n, 0, 0)),
        out_shape=jax.ShapeDtypeStruct((b, d // HEAD_PAIR, l, HEAD_PAIR), BF16),
        scratch_shapes=[
            pltpu.VMEM((PIPE_SLOTS, SWA_KB, SWA_UNIT_HEADS * SWA_QB), F32),
            pltpu.VMEM((PIPE_SLOTS, c, SWA_UNIT_HEADS * SWA_QB), F32),
            pltpu.VMEM((PIPE_SLOTS, 1, SWA_UNIT_HEADS * SWA_QB), F32),
        ],
        compiler_params=_params("arbitrary", "arbitrary"),
        name="sliding_window_attention",
    )(qt, k, vt, kc, vct, jnp.asarray(_swa_mask_tables()), sink_rows)


def _post_kernel(a_ref, h_ref, mod_ref, g_ref, wo_ref, w1_ref, w2_ref, o_ref, *, ff_chunk):
    mod = mod_ref[0]
    a = jnp.concatenate([a_ref[0, p] for p in range(a_ref.shape[1])], axis=1)
    y = jnp.dot(a, wo_ref[0], preferred_element_type=F32)
    h1 = h_ref[0] + mod[2:3] * y
    u = _rms_modulate(h1, g_ref[...], mod[3:4], mod[4:5]).astype(BF16)
    d_ff = w1_ref.shape[2]
    acc = jnp.zeros_like(h1)
    for c0 in range(0, d_ff, ff_chunk):
        t = jnp.dot(u, w1_ref[0, :, c0:c0 + ff_chunk], preferred_element_type=F32)
        r = jnp.square(jnp.maximum(t, 0.0)).astype(BF16)
        acc = acc + jnp.dot(r, w2_ref[0, c0:c0 + ff_chunk, :], preferred_element_type=F32)
    o_ref[0] = h1 + mod[5:6] * acc


def _post_block(a, h, mod, g, wo, w1, w2, layer, *, tm, ff_chunk=1024):
    b, t, d = h.shape

    def layer_spec(w, index):
        return pl.BlockSpec((1,) + w.shape[1:], lambda i, j: (index, 0, 0), pipeline_mode=pl.Buffered(1))

    return pl.pallas_call(
        functools.partial(_post_kernel, ff_chunk=ff_chunk),
        grid=(b, t // tm),
        in_specs=[
            pl.BlockSpec((1, d // HEAD_PAIR, tm, HEAD_PAIR), lambda i, j: (i, 0, j, 0)),
            pl.BlockSpec((1, tm, d), lambda i, j: (i, j, 0)),
            pl.BlockSpec((1, 6, d), lambda i, j: (i, 0, 0)),
            _const_spec((1, d)),
            layer_spec(wo, 0),
            layer_spec(w1, layer),
            layer_spec(w2, layer),
        ],
        out_specs=pl.BlockSpec((1, tm, d), lambda i, j: (i, j, 0)),
        out_shape=jax.ShapeDtypeStruct((b, t, d), F32),
        compiler_params=_params("arbitrary", "arbitrary"),
        name="outproj_mlp",
    )(a, h, mod, g, wo, w1, w2)


def _rope_tables_t(l):
    t = np.arange(l)
    n_freq = HEAD_DIM // 4
    inv = ROPE_BASE ** (-np.arange(n_freq, dtype=np.float64) / n_freq)
    ang = np.stack([(t // GRID_W)[:, None] * inv, (t % GRID_W)[:, None] * inv], axis=1)
    cos = np.transpose(np.cos(ang), (1, 2, 0))
    sin = np.transpose(np.sin(ang), (1, 2, 0))
    cos_t = np.stack([cos, cos], axis=1).reshape(HEAD_DIM, l)
    sin_t = np.stack([-sin, sin], axis=1).reshape(HEAD_DIM, l)
    return jnp.asarray(cos_t, F32), jnp.asarray(sin_t, F32)


def _swap_rotary_halves(v):
    q = HEAD_DIM // 4
    return jnp.concatenate([v[q:2 * q], v[0:q], v[3 * q:4 * q], v[2 * q:3 * q]], axis=0)


def _gain_rope_tables(gain, cos_t, sin_t):
    gain = gain.astype(F32)
    return ((gain[:, None] * cos_t).astype(BF16), (_swap_rotary_halves(gain)[:, None] * sin_t).astype(BF16))


def _gain_cols(gain, scale=1.0):
    return jnp.broadcast_to((gain.astype(F32) * scale)[:, None], (HEAD_DIM, QKV_SUB))


def kernel(x, c, ctx, c_ctx, ada_w, ada_b, g_mix, g_mlp, mlp_w1, mlp_w2, na_wqkv, na_q_gain, na_k_gain,
           na_rpb, na_wo, swa_wqkv, swa_q_gain, swa_k_gain, swa_sink, swa_wo):
    b, l, d = x.shape
    n_ctx = ctx.shape[1]
    tm = TM_POST
    tmc = b * n_ctx
    ctx_flat = ctx.reshape(1, tmc, d)
    scale = HEAD_DIM ** -0.5 * LOG2E

    cond =jnp.concatenate([c, c_ctx[None], jnp.zeros((8 - b - 1, d), F32)], axis=0)
    mods = _modulation(cond, ada_w, ada_b)

    def mods_of(i):
        lat = mods[i, :b].reshape(b, 6, d)
        cx = mods[i, b].reshape(1, 6, d)
        return lat, cx

    def row(v):
        return v.reshape(1, d)

    mod_lat, mod_ctx = mods_of(0)
    gq, gk = na_q_gain[0], na_k_gain[0]
    gains = (_gain_cols(gq, scale), _gain_cols(gk))
    qt, k, vt, wt = _qkv_project(x, mod_lat, row(g_mix[0]), na_wqkv[0], gains, None, kv_width=d, tm=TM_QKV)
    qct, kc, vct = _qkv_project(ctx_flat, mod_ctx, row(g_mix[0]), wt, gains, None, kv_width=d, tm=tmc)
    a_lat = _na_attention(qt, k, vt, kc, vct, _na_bias_rows(na_rpb[0] * LOG2E))
    a_ctx = _ctx_attention(qct, kc, vct, b)
    w1 = mlp_w1.astype(BF16)
    w2 = mlp_w2.astype(BF16)
    wo = na_wo.astype(BF16)
    h_lat = _post_block(a_lat, x, mod_lat, row(g_mlp[0]), wo, w1, w2, 0, tm=tm)
    h_ctx = _post_block(a_ctx, ctx_flat, mod_ctx, row(g_mlp[0]), wo, w1, w2, 0, tm=tmc)

    mod_lat, mod_ctx = mods_of(1)
    kvw = SWA_KV_HEADS * HEAD_DIM
    gq, gk = swa_q_gain[0], swa_k_gain[0]
    cos_t, sin_t = _rope_tables_t(l)
    rope_tabs = (_gain_rope_tables(gq * scale, cos_t, sin_t) + _gain_rope_tables(gk, cos_t, sin_t))
    qt, k, vt, wt = _qkv_project(h_lat, mod_lat, row(g_mix[1]), swa_wqkv[0], None, rope_tabs, kv_width=kvw,
                                 tm=TM_QKV)
    gains = (_gain_cols(gq, scale), _gain_cols(gk))
    _, kc, vct = _qkv_project(h_ctx, mod_ctx, row(g_mix[1]), wt, gains, None, kv_width=kvw, tm=tmc)
    sink_rows = jnp.repeat(swa_sink[0].astype(F32) * LOG2E, SWA_QB).reshape(SWA_KV_HEADS, 1, SWA_GROUP * SWA_QB)
    a_lat = _swa_attention(qt, k, vt, kc, vct, sink_rows)
    return _post_block(a_lat, h_lat, mod_lat, row(g_mlp[1]), swa_wo.astype(BF16), w1, w2, 1, tm=tm)
```

```python
import functools

import numpy as np
import jax
import jax.numpy as jnp
from jax import lax
from jax.experimental import pallas as pl
from jax.experimental.pallas import tpu as pltpu

GRID_W = 64
N_HEADS = 16
HEAD_DIM = 64
NA_KH = 8
NA_KW = 16
SWA_KV_HEADS = 4
SWA_GROUP = N_HEADS // SWA_KV_HEADS
SWA_WINDOW = 128
ROPE_BASE = 10000.0
NORM_EPS = 1e-6
NEG_INF = -1e30
LOG2E = float(np.log2(np.e))

F32 = jnp.float32
BF16 = jnp.bfloat16

V7X_VMEM_BYTES = 64 * 1024 * 1024
VMEM_LIMIT_BYTES = V7X_VMEM_BYTES - 8 * 1024 * 1024

BF16_SUBLANES = 16
HEAD_PAIR = 2 * HEAD_DIM
NA_Q_ROWS = 4
NA_K_ROWS = NA_Q_ROWS + NA_KH
NA_QB = NA_Q_ROWS * GRID_W
NA_KB = NA_K_ROWS * GRID_W
SWA_QB = 128
SWA_KB = SWA_QB + 2 * SWA_WINDOW
SWA_UNIT_HEADS = 2
TM_QKV = 1024
TM_POST = 1024
QKV_SUB = 256
QKV_CHUNK = 256
QKV_CHUNK_ROPE = 1024
PIPE_AHEAD = 2
SWA_PIPE_AHEAD = 3
PIPE_SLOTS = 4


def _params(*semantics):
    return pltpu.CompilerParams(dimension_semantics=semantics, vmem_limit_bytes=VMEM_LIMIT_BYTES)


def _const_spec(shape):
    zeros = (0,) * len(shape)
    return pl.BlockSpec(shape, lambda *_: zeros, pipeline_mode=pl.Buffered(1))


def _mod_kernel(cond_ref, w_ref, b_ref, o_ref):
    s = jax.nn.silu(cond_ref[...]).astype(BF16)
    o_ref[0] = jnp.dot(s, w_ref[0].astype(BF16), preferred_element_type=F32) + b_ref[0]


def _modulation(cond, ada_w, ada_b, tn=1536):
    depth, d, n = ada_w.shape
    rows = cond.shape[0]
    return pl.pallas_call(
        _mod_kernel,
        grid=(depth, n // tn),
        in_specs=[
            pl.BlockSpec((rows, d), lambda i, j: (0, 0)),
            pl.BlockSpec((1, d, tn), lambda i, j: (i, 0, j)),
            pl.BlockSpec((1, 1, tn), lambda i, j: (i, 0, j)),
        ],
        out_specs=pl.BlockSpec((1, rows, tn), lambda i, j: (i, 0, j)),
        out_shape=jax.ShapeDtypeStruct((depth, rows, n), F32),
        compiler_params=_params("arbitrary", "arbitrary"),
        name="adaln_modulation",
    )(cond, ada_w, ada_b.reshape(depth, 1, n))


def _rms_modulate(x, g, shift, scale):
    y = x * lax.rsqrt(jnp.mean(x * x, axis=-1, keepdims=True) + NORM_EPS) * g
    return y * (1.0 + scale) + shift


def _head_norm_t(a, gain_b):
    n = a.shape[0] // HEAD_DIM
    tm = a.shape[1]
    a = a.reshape(n, HEAD_DIM, tm)
    a = (a * lax.rsqrt(jnp.mean(a * a, axis=1, keepdims=True) + NORM_EPS)).astype(BF16) * gain_b[None]
    return a.reshape(n * HEAD_DIM, tm)


def _head_norm_rope_t(a, gain_cos, gain_sin):
    n = a.shape[0] // HEAD_DIM
    tm = a.shape[1]
    a = a.reshape(n, HEAD_DIM, tm)
    a = (a * lax.rsqrt(jnp.mean(a * a, axis=1, keepdims=True) + NORM_EPS)).astype(BF16)
    q = HEAD_DIM // 4
    swapped = jnp.concatenate([a[:, q:2 * q], a[:, 0:q], a[:, 3 * q:4 * q], a[:, 2 * q:3 * q]], axis=1)
    a = a * gain_cos[None] + swapped * gain_sin[None]
    return a.reshape(n * HEAD_DIM, tm)


def _qkv_kernel(*refs, d_model, kv_width, use_rope, chunk, sub, prepare_weight):
    if prepare_weight:
        *refs, wt_ref = refs
    if use_rope:
        h_ref, mod_ref, g_ref, w_ref, qc_ref, qs_ref, kc_ref, ks_ref, q_ref, k_ref, v_ref = refs
    else:
        h_ref, mod_ref, g_ref, w_ref, gq_ref, gk_ref, q_ref, k_ref, v_ref = refs
        gq = gq_ref[...].astype(BF16)
        gk = gk_ref[...].astype(BF16)
    if prepare_weight:
        @pl.when((pl.program_id(0) == 0) & (pl.program_id(1) == 0))
        def _():
            for c0 in range(0, wt_ref.shape[1], QKV_SUB):
                wt_ref[:, c0:c0 + QKV_SUB] = w_ref[:, c0:c0 + QKV_SUB].astype(BF16)
    else:
        wt_ref = w_ref
    mod = mod_ref[0]
    g_eff = (g_ref[...] * (1.0 + mod[1:2])).astype(BF16)
    shift = mod[0:1].astype(BF16)
    contract_last = (((1,), (1,)), ((), ()))
    kv_chunk = min(chunk, kv_width)
    for t0 in range(0, h_ref.shape[1], sub):
        toks = slice(t0, t0 + sub)
        x = h_ref[0, toks, :]
        xn = (x * lax.rsqrt(jnp.mean(x * x, axis=-1, keepdims=True) + NORM_EPS)).astype(BF16)
        u = xn * g_eff + shift

        def proj(r0, rows):
            return jnp.dot(u, wt_ref[:, r0:r0 + rows], preferred_element_type=F32).T

        def norm_q(a):
            if use_rope:
                return _head_norm_rope_t(a, qc_ref[:, toks], qs_ref[:, toks])
            return _head_norm_t(a, gq)

        def norm_k(a):
            if use_rope:
                return _head_norm_rope_t(a, kc_ref[:, toks], ks_ref[:, toks])
            return _head_norm_t(a, gk)

        for r0 in range(0, d_model, chunk):
            q_ref[0, r0:r0 + chunk, toks] = norm_q(proj(r0, chunk))
        for r0 in range(0, kv_width, kv_chunk):
            kt = norm_k(proj(d_model + r0, kv_chunk)).T
            for c0 in range(0, kv_chunk, HEAD_PAIR):
                k_ref[0, (r0 + c0) // HEAD_PAIR, toks, :] = kt[:, c0:c0 + HEAD_PAIR]
        for r0 in range(0, kv_width, kv_chunk):
            v_ref[0, r0:r0 + kv_chunk, toks] = proj(d_model + kv_width + r0, kv_chunk).astype(BF16)


def _qkv_project(h, mod, g, w, gains, rope_tabs, *, kv_width, tm, sub=QKV_SUB):
    b, t, d = h.shape
    use_rope = rope_tabs is not None
    prepare_weight = w.dtype == F32
    n_out = d + 2 * kv_width
    assert w.shape == (d, n_out)
    chunk = QKV_CHUNK_ROPE if use_rope else QKV_CHUNK
    in_specs = [
        pl.BlockSpec((1, tm, d), lambda i, j: (i, j, 0)),
        pl.BlockSpec((1, 6, d), lambda i, j: (i, 0, 0)),
        _const_spec((1, d)),
        _const_spec(w.shape),
    ]
    args = [h, mod, g, w]
    if use_rope:
        in_specs += [pl.BlockSpec((HEAD_DIM, tm), lambda i, j: (0, j))] * 4
        args += list(rope_tabs)
    else:
        in_specs += [_const_spec((HEAD_DIM, sub))] * 2
        args += list(gains)
    out_specs = [
        pl.BlockSpec((1, d, tm), lambda i, j: (i, 0, j)),
        pl.BlockSpec((1, kv_width // HEAD_PAIR, tm, HEAD_PAIR), lambda i, j: (i, 0, j, 0)),
        pl.BlockSpec((1, kv_width, tm), lambda i, j: (i, 0, j)),
    ]
    out_shape = [
        jax.ShapeDtypeStruct((b, d, t), BF16),
        jax.ShapeDtypeStruct((b, kv_width // HEAD_PAIR, t, HEAD_PAIR), BF16),
        jax.ShapeDtypeStruct((b, kv_width, t), BF16),
    ]
    if prepare_weight:
        out_specs.append(_const_spec((d, n_out)))
        out_shape.append(jax.ShapeDtypeStruct((d, n_out), BF16))
    return pl.pallas_call(
        functools.partial(_qkv_kernel, d_model=d, kv_width=kv_width, use_rope=use_rope, chunk=chunk,
                          sub=sub, prepare_weight=prepare_weight),
        grid=(b, t // tm),
        in_specs=in_specs,
        out_specs=out_specs,
        out_shape=out_shape,
        compiler_params=_params("arbitrary", "arbitrary"),
        name="qkv_rope" if use_rope else "qkv",
    )(*args)


def _col_max(*parts):
    m = parts[0].max(axis=0, keepdims=True)
    for p in parts[1:]:
        m = jnp.maximum(m, p.max(axis=0, keepdims=True))
    return m


def _mask_rows_to_head(q_pair, head):
    row = lax.broadcasted_iota(jnp.int32, q_pair.shape, 0)
    return jnp.where((row // HEAD_DIM) == head, q_pair, jnp.zeros_like(q_pair))


def _na_build_bias(rows_ref, bias_ref):
    w = GRID_W
    kc = lax.broadcasted_iota(jnp.int32, (w, 2 * w), 0)
    qc = lax.broadcasted_iota(jnp.int32, (w, 2 * w), 1) % w
    c0 = jnp.clip(qc - NA_KW // 2, 0, w - NA_KW)
    col_valid = (kc >= c0) & (kc < c0 + NA_KW)
    plan, combos = _na_bias_plan()
    for hh in range(2):
        tiles = []
        for n in range(len(combos)):
            row = jnp.broadcast_to(rows_ref[hh, n:n + 1, :], (w, 2 * w))
            tiles.append(jnp.where(col_valid, pltpu.roll(row, 0, 1, stride=1, stride_axis=0), NEG_INF))
        for v in range(3):
            for kr in range(NA_K_ROWS):
                for half in range(NA_Q_ROWS // 2):
                    bias_ref[hh, v, kr * w:(kr + 1) * w, half * 2 * w:(half + 1) * 2 * w] = tiles[plan[v, kr, half]]


def _na_kernel(q_ref, k_ref, v_ref, kc_ref, vc_ref, rows_ref, o_ref, bias_ref, sl_ref, sc_ref, m_ref, *,
               n_blocks, grid_rows, unroll):
    @pl.when(pl.program_id(1) == 0)
    def _():
        _na_build_bias(rows_ref, bias_ref)

    ones_loc = jnp.ones((BF16_SUBLANES, NA_KB), BF16)
    ones_ctx = jnp.ones((BF16_SUBLANES, kc_ref.shape[2]), BF16)

    def window(i):
        start_row = jnp.clip(NA_Q_ROWS * i - NA_KH // 2, 0, grid_rows - NA_K_ROWS)
        return pl.multiple_of(start_row * GRID_W, NA_QB), pl.multiple_of(i * NA_QB, NA_QB)

    def scores(i, hh, slot):
        i = jnp.minimum(i, n_blocks - 1)
        k0, q0 = window(i)
        variant = jnp.where(i == 0, 0, jnp.where(i == n_blocks - 1, 2, 1))
        qm = _mask_rows_to_head(q_ref[0, :, pl.ds(q0, NA_QB)], hh)
        s_loc = (jnp.dot(k_ref[0, 0, pl.ds(k0, NA_KB), :], qm, preferred_element_type=F32)
                 + bias_ref[hh, variant])
        s_ctx = jnp.dot(kc_ref[0, 0], qm, preferred_element_type=F32)
        sl_ref[slot] = s_loc
        sc_ref[slot] = s_ctx
        m_ref[slot] = _col_max(s_loc, s_ctx)

    def attend(i, hh, slot):
        k0, _ = window(i)
        m = m_ref[slot]
        p_loc = jnp.exp2(sl_ref[slot] - m).astype(BF16)
        p_ctx = jnp.exp2(sc_ref[slot] - m).astype(BF16)
        rows = slice(hh * HEAD_DIM, (hh + 1) * HEAD_DIM)
        v_loc = jnp.concatenate([v_ref[0, rows, pl.ds(k0, NA_KB)], ones_loc], axis=0)
        v_ctx = jnp.concatenate([vc_ref[0, rows, :], ones_ctx], axis=0)
        o = (jnp.dot(v_loc, p_loc, preferred_element_type=F32)
             + jnp.dot(v_ctx, p_ctx, preferred_element_type=F32))
        return o[:HEAD_DIM] / o[HEAD_DIM:HEAD_DIM + 1]

    for n in range(PIPE_AHEAD):
        scores(n // 2, n % 2, n)

    single_trip = n_blocks == unroll

    def body(it, carry):
        base = it * unroll
        units = [(base + blk, hh) for blk in range(unroll + (PIPE_AHEAD + 1) // 2) for hh in range(2)]
        pair = []
        for n, (i, hh) in enumerate(units[:2 * unroll]):
            ahead = units[n + PIPE_AHEAD]
            if not (single_trip and ahead[0] >= n_blocks):
                scores(ahead[0], ahead[1], (n + PIPE_AHEAD) % PIPE_SLOTS)
            pair.append(attend(i, hh, n % PIPE_SLOTS))
            if hh == 1:
                q0 = pl.multiple_of(i * NA_QB, NA_QB)
                o_ref[0, 0, pl.ds(q0, NA_QB), :] = jnp.concatenate(pair, axis=0).T.astype(BF16)
                pair = []
        return carry

    if single_trip:
        body(0, 0)
    else:
        lax.fori_loop(0, n_blocks // unroll, body, 0)


def _na_attention(qt, k, vt, kc, vct, bias_rows, unroll=16):
    b, d, l = qt.shape
    c = vct.shape[2] // b
    pairs = d // HEAD_PAIR
    grid_rows = l // GRID_W
    n_blocks = grid_rows // NA_Q_ROWS
    n_combos = bias_rows.shape[1]
    assert grid_rows >= NA_K_ROWS and n_blocks >= 3 and n_blocks % unroll == 0
    assert (2 * unroll) % PIPE_SLOTS == 0 and PIPE_SLOTS > PIPE_AHEAD
    return pl.pallas_call(
        functools.partial(_na_kernel, n_blocks=n_blocks, grid_rows=grid_rows, unroll=unroll),
        grid=(pairs, b),
        in_specs=[
            pl.BlockSpec((1, HEAD_PAIR, l), lambda p, i: (i, p, 0)),
            pl.BlockSpec((1, 1, l, HEAD_PAIR), lambda p, i: (i, p, 0, 0)),
            pl.BlockSpec((1, HEAD_PAIR, l), lambda p, i: (i, p, 0)),
            pl.BlockSpec((1, 1, c, HEAD_PAIR), lambda p, i: (0, p, i, 0)),
            pl.BlockSpec((1, HEAD_PAIR, c), lambda p, i: (0, p, i)),
            pl.BlockSpec((2, n_combos, 2 * GRID_W), lambda p, i: (p, 0, 0)),
        ],
        out_specs=pl.BlockSpec((1, 1, l, HEAD_PAIR), lambda p, i: (i, p, 0, 0)),
        out_shape=jax.ShapeDtypeStruct((b, pairs, l, HEAD_PAIR), BF16),
        scratch_shapes=[
            pltpu.VMEM((2, 3, NA_KB, NA_QB), F32),
            pltpu.VMEM((PIPE_SLOTS, NA_KB, NA_QB), F32),
            pltpu.VMEM((PIPE_SLOTS, c, NA_QB), F32),
            pltpu.VMEM((PIPE_SLOTS, 1, NA_QB), F32),
        ],
        compiler_params=_params("arbitrary", "arbitrary"),
        name="neighbourhood_attention",
    )(qt, k, vt, kc, vct, bias_rows)


NA_MASKED_ROW = 2 * NA_KH - 1


@functools.lru_cache(maxsize=None)
def _na_bias_plan():
    kr = np.arange(NA_K_ROWS)[:, None]
    qr = np.arange(NA_Q_ROWS)[None, :]
    top = NA_KH - 1
    variants = [
        (kr < NA_KH + 0 * qr, kr - qr + top),
        ((kr >= qr) & (kr < qr + NA_KH), kr - qr + top - NA_KH // 2),
        (kr >= NA_K_ROWS - NA_KH + 0 * qr, kr - qr + top - (NA_K_ROWS - NA_Q_ROWS)),
    ]
    idx = np.stack([np.where(valid, off, NA_MASKED_ROW) for valid, off in variants])
    assert idx.min() >= 0 and idx.max() <= NA_MASKED_ROW
    combos = []
    plan = np.zeros((3, NA_K_ROWS, NA_Q_ROWS // 2), np.int32)
    for v in range(3):
        for r in range(NA_K_ROWS):
            for half in range(NA_Q_ROWS // 2):
                key = (int(idx[v, r, 2 * half]), int(idx[v, r, 2 * half + 1]))
                if key not in combos:
                    combos.append(key)
                plan[v, r, half] = combos.index(key)
    return plan, tuple(combos)


def _na_bias_rows(rpb):
    h, n_rows, n_off = rpb.shape
    w = GRID_W
    reach = NA_KW - 1
    _, combos = _na_bias_plan()
    masked = n_rows * n_off
    lane = np.arange(2 * w)
    first = np.where(lane <= reach, reach - lane, np.where(lane >= 2 * w - reach, reach + 2 * w - lane, -1))
    second = np.where(np.abs(lane - w) <= reach, reach + w - lane, -1)
    idx = np.full((len(combos), 2 * w), masked, np.int32)
    for n, (a, b) in enumerate(combos):
        if a != NA_MASKED_ROW:
            idx[n] = np.where(first >= 0, a * n_off + first, idx[n])
        if b != NA_MASKED_ROW:
            idx[n] = np.where(second >= 0, b * n_off + second, idx[n])
    flat = jnp.concatenate([rpb.reshape(h, masked), jnp.full((h, 1), NEG_INF, rpb.dtype)], axis=1)
    return flat[:, idx]


def _ctx_kernel(q_ref, k_ref, v_ref, o_ref, *, pairs):
    def scores(p, hh):
        cols = slice(p * HEAD_PAIR, (p + 1) * HEAD_PAIR)
        return jnp.dot(k_ref[0, p], _mask_rows_to_head(q_ref[0, cols, :], hh),
                       preferred_element_type=F32)

    heads = [(p, hh) for p in range(pairs) for hh in range(2)]
    s_next = scores(*heads[0])
    outs = []
    for n, (p, hh) in enumerate(heads):
        s = s_next
        if n + 1 < len(heads):
            s_next = scores(*heads[n + 1])
        pr = jnp.exp2(s - s.max(axis=0, keepdims=True))
        l = pr.sum(axis=0, keepdims=True)
        rows = slice(p * HEAD_PAIR + hh * HEAD_DIM, p * HEAD_PAIR + (hh + 1) * HEAD_DIM)
        outs.append(jnp.dot(v_ref[0, rows, :], pr.astype(BF16), preferred_element_type=F32) / l)
        if hh == 1:
            o_ref[0, p] = jnp.concatenate(outs, axis=0).T.astype(BF16)
            outs = []


def _ctx_attention(qct, kc, vct, batches):
    _, d, bc = qct.shape
    c = bc // batches
    pairs = d // HEAD_PAIR
    return pl.pallas_call(
        functools.partial(_ctx_kernel, pairs=pairs),
        grid=(batches,),
        in_specs=[
            pl.BlockSpec((1, d, c), lambda i: (0, 0, i)),
            pl.BlockSpec((1, pairs, c, HEAD_PAIR), lambda i: (0, 0, i, 0)),
            pl.BlockSpec((1, d, c), lambda i: (0, 0, i)),
        ],
        out_specs=pl.BlockSpec((1, pairs, c, HEAD_PAIR), lambda i: (0, 0, i, 0)),
        out_shape=jax.ShapeDtypeStruct((1, pairs, bc, HEAD_PAIR), BF16),
        compiler_params=_params("arbitrary"),
        name="context_attention",
    )(qct, kc, vct)


def _swa_kernel(q_ref, k_ref, v_ref, kc_ref, vc_ref, mask_ref, sink_ref, o_ref, sl_ref, sc_ref, m_ref, *,
                n_blocks, seq, unroll):
    kv_in_pair = pl.program_id(1) % 2
    ones_loc = jnp.ones((BF16_SUBLANES, SWA_KB), BF16)
    ones_ctx = jnp.ones((BF16_SUBLANES, kc_ref.shape[2]), BF16)

    def window(j):
        k0 = pl.multiple_of(jnp.clip(SWA_QB * j - SWA_WINDOW, 0, seq - SWA_KB), SWA_QB)
        return k0, pl.multiple_of(j * SWA_QB, SWA_QB)

    hp = SWA_UNIT_HEADS
    parts = SWA_GROUP // hp

    def scores(j, part, slot):
        j = jnp.minimum(j, n_blocks - 1)
        k0, q0 = window(j)
        variant = jnp.where(j == 0, 0, jnp.where(j == n_blocks - 1, 2, 1))
        heads = range(part * hp, (part + 1) * hp)
        qg = jnp.concatenate(
            [q_ref[0, g * HEAD_DIM:(g + 1) * HEAD_DIM, pl.ds(q0, SWA_QB)] for g in heads], axis=1)
        qm = _mask_rows_to_head(jnp.concatenate([qg, qg], axis=0), kv_in_pair)
        mask = mask_ref[variant]
        s_loc = (jnp.dot(k_ref[0, 0, pl.ds(k0, SWA_KB), :], qm, preferred_element_type=F32)
                 + jnp.concatenate([mask] * hp, axis=1))
        s_ctx = jnp.dot(kc_ref[0, 0], qm, preferred_element_type=F32)
        sl_ref[slot] = s_loc
        sc_ref[slot] = s_ctx
        m_ref[slot] = jnp.maximum(_col_max(s_loc, s_ctx), sink_ref[0, :, part * hp * SWA_QB:(part + 1) * hp * SWA_QB])

    def attend(j, part, slot):
        k0, q0 = window(j)
        m = m_ref[slot]
        sink = sink_ref[0, :, part * hp * SWA_QB:(part + 1) * hp * SWA_QB]
        p_loc = jnp.exp2(sl_ref[slot] - m).astype(BF16)
        p_ctx = jnp.exp2(sc_ref[slot] - m).astype(BF16)
        v_loc = jnp.concatenate([v_ref[0, :, pl.ds(k0, SWA_KB)], ones_loc], axis=0)
        v_ctx = jnp.concatenate([vc_ref[0], ones_ctx], axis=0)
        o = (jnp.dot(v_loc, p_loc, preferred_element_type=F32)
             + jnp.dot(v_ctx, p_ctx, preferred_element_type=F32))
        o = o[:HEAD_DIM] / (o[HEAD_DIM:HEAD_DIM + 1] + jnp.exp2(sink - m))
        o_rows = jnp.concatenate([o[:, g * SWA_QB:(g + 1) * SWA_QB] for g in range(hp)], axis=0)
        o_ref[0, part, pl.ds(q0, SWA_QB), :] = o_rows.T.astype(BF16)

    for n in range(SWA_PIPE_AHEAD):
        scores(n // parts, n % parts, n)

    def body(it, carry):
        base = it * unroll
        for n in range(unroll * parts):
            ahead = n + SWA_PIPE_AHEAD
            scores(base + ahead // parts, ahead % parts, ahead % PIPE_SLOTS)
            attend(base + n // parts, n % parts, n % PIPE_SLOTS)
        return carry

    lax.fori_loop(0, n_blocks // unroll, body, 0)


def _swa_mask_tables():
    kr = np.arange(SWA_KB)[:, None]
    qc = np.arange(SWA_QB)[None, :]
    rel = [kr - qc, kr - SWA_WINDOW - qc, kr - (SWA_KB - SWA_QB) - qc]
    return np.stack([np.where(np.abs(r) <= SWA_WINDOW, 0.0, NEG_INF) for r in rel]).astype(np.float32)


def _swa_attention(qt, k, vt, kc, vct, sink_rows, unroll=32):
    b, d, l = qt.shape
    c = vct.shape[2] // b
    kv = vt.shape[1] // HEAD_DIM
    gw = SWA_GROUP * HEAD_DIM
    parts = SWA_GROUP // SWA_UNIT_HEADS
    n_blocks = l // SWA_QB
    assert SWA_UNIT_HEADS * HEAD_DIM == HEAD_PAIR
    assert l >= SWA_KB and n_blocks >= 3 and n_blocks % unroll == 0
    assert (unroll * SWA_GROUP // SWA_UNIT_HEADS) % PIPE_SLOTS == 0 and PIPE_SLOTS > SWA_PIPE_AHEAD
    return pl.pallas_call(
        functools.partial(_swa_kernel, n_blocks=n_blocks, seq=l, unroll=unroll),
        grid=(b, kv),
        in_specs=[
            pl.BlockSpec((1, gw, l), lambda i, n: (i, n, 0)),
            pl.BlockSpec((1, 1, l, HEAD_PAIR), lambda i, n: (i, n // 2, 0, 0)),
            pl.BlockSpec((1, HEAD_DIM, l), lambda i, n: (i, n, 0)),
            pl.BlockSpec((1, 1, c, HEAD_PAIR), lambda i, n: (0, n // 2, i, 0)),
            pl.BlockSpec((1, HEAD_DIM, c), lambda i, n: (0, n, i)),
            _const_spec((3, SWA_KB, SWA_QB)),
            pl.BlockSpec((1, 1, SWA_GROUP * SWA_QB), lambda i, n: (n, 0, 0)),
        ],
        out_specs=pl.BlockSpec((1, parts, l, HEAD_PAIR), lambda i, n: (i, n, 0, 0)),
        out_shape=jax.ShapeDtypeStruct((b, d // HEAD_PAIR, l, HEAD_PAIR), BF16),
        scratch_shapes=[
            pltpu.VMEM((PIPE_SLOTS, SWA_KB, SWA_UNIT_HEADS * SWA_QB), F32),
            pltpu.VMEM((PIPE_SLOTS, c, SWA_UNIT_HEADS * SWA_QB), F32),
            pltpu.VMEM((PIPE_SLOTS, 1, SWA_UNIT_HEADS * SWA_QB), F32),
        ],
        compiler_params=_params("arbitrary", "arbitrary"),
        name="sliding_window_attention",
    )(qt, k, vt, kc, vct, jnp.asarray(_swa_mask_tables()), sink_rows)


def _post_kernel(a_ref, h_ref, mod_ref, g_ref, wo_ref, w1_ref, w2_ref, o_ref, *, ff_chunk):
    mod = mod_ref[0]
    a = jnp.concatenate([a_ref[0, p] for p in range(a_ref.shape[1])], axis=1)
    y = jnp.dot(a, wo_ref[0], preferred_element_type=F32)
    h1 = h_ref[0] + mod[2:3] * y
    u = _rms_modulate(h1, g_ref[...], mod[3:4], mod[4:5]).astype(BF16)
    d_ff = w1_ref.shape[2]
    acc = jnp.zeros_like(h1)
    for c0 in range(0, d_ff, ff_chunk):
        t = jnp.dot(u, w1_ref[0, :, c0:c0 + ff_chunk], preferred_element_type=F32)
        r = jnp.square(jnp.maximum(t, 0.0)).astype(BF16)
        acc = acc + jnp.dot(r, w2_ref[0, c0:c0 + ff_chunk, :], preferred_element_type=F32)
    o_ref[0] = h1 + mod[5:6] * acc


def _post_block(a, h, mod, g, wo, w1, w2, layer, *, tm, ff_chunk=1024):
    b, t, d = h.shape

    def layer_spec(w, index):
        return pl.BlockSpec((1,) + w.shape[1:], lambda i, j: (index, 0, 0), pipeline_mode=pl.Buffered(1))

    return pl.pallas_call(
        functools.partial(_post_kernel, ff_chunk=ff_chunk),
        grid=(b, t // tm),
        in_specs=[
            pl.BlockSpec((1, d // HEAD_PAIR, tm, HEAD_PAIR), lambda i, j: (i, 0, j, 0)),
            pl.BlockSpec((1, tm, d), lambda i, j: (i, j, 0)),
            pl.BlockSpec((1, 6, d), lambda i, j: (i, 0, 0)),
            _const_spec((1, d)),
            layer_spec(wo, 0),
            layer_spec(w1, layer),
            layer_spec(w2, layer),
        ],
        out_specs=pl.BlockSpec((1, tm, d), lambda i, j: (i, j, 0)),
        out_shape=jax.ShapeDtypeStruct((b, t, d), F32),
        compiler_params=_params("arbitrary", "arbitrary"),
        name="outproj_mlp",
    )(a, h, mod, g, wo, w1, w2)


def _rope_tables_t(l):
    t = np.arange(l)
    n_freq = HEAD_DIM // 4
    inv = ROPE_BASE ** (-np.arange(n_freq, dtype=np.float64) / n_freq)
    ang = np.stack([(t // GRID_W)[:, None] * inv, (t % GRID_W)[:, None] * inv], axis=1)
    cos = np.transpose(np.cos(ang), (1, 2, 0))
    sin = np.transpose(np.sin(ang), (1, 2, 0))
    cos_t = np.stack([cos, cos], axis=1).reshape(HEAD_DIM, l)
    sin_t = np.stack([-sin, sin], axis=1).reshape(HEAD_DIM, l)
    return jnp.asarray(cos_t, F32), jnp.asarray(sin_t, F32)


def _swap_rotary_halves(v):
    q = HEAD_DIM // 4
    return jnp.concatenate([v[q:2 * q], v[0:q], v[3 * q:4 * q], v[2 * q:3 * q]], axis=0)


def _gain_rope_tables(gain, cos_t, sin_t):
    gain = gain.astype(F32)
    return ((gain[:, None] * cos_t).astype(BF16), (_swap_rotary_halves(gain)[:, None] * sin_t).astype(BF16))


def _gain_cols(gain, scale=1.0):
    return jnp.broadcast_to((gain.astype(F32) * scale)[:, None], (HEAD_DIM, QKV_SUB))


def kernel(x, c, ctx, c_ctx, ada_w, ada_b, g_mix, g_mlp, mlp_w1, mlp_w2, na_wqkv, na_q_gain, na_k_gain,
           na_rpb, na_wo, swa_wqkv, swa_q_gain, swa_k_gain, swa_sink, swa_wo):
    b, l, d = x.shape
    n_ctx = ctx.shape[1]
    tm = TM_POST
    tmc = b * n_ctx
    ctx_flat = ctx.reshape(1, tmc, d)
    scale = HEAD_DIM ** -0.5 * LOG2E

    cond =jnp.concatenate([c, c_ctx[None], jnp.zeros((8 - b - 1, d), F32)], axis=0)
    mods = _modulation(cond, ada_w, ada_b)

    def mods_of(i):
        lat = mods[i, :b].reshape(b, 6, d)
        cx = mods[i, b].reshape(1, 6, d)
        return lat, cx

    def row(v):
        return v.reshape(1, d)

    mod_lat, mod_ctx = mods_of(0)
    gq, gk = na_q_gain[0], na_k_gain[0]
    gains = (_gain_cols(gq, scale), _gain_cols(gk))
    qt, k, vt, wt = _qkv_project(x, mod_lat, row(g_mix[0]), na_wqkv[0], gains, None, kv_width=d, tm=TM_QKV)
    qct, kc, vct = _qkv_project(ctx_flat, mod_ctx, row(g_mix[0]), wt, gains, None, kv_width=d, tm=tmc)
    a_lat = _na_attention(qt, k, vt, kc, vct, _na_bias_rows(na_rpb[0] * LOG2E))
    a_ctx = _ctx_attention(qct, kc, vct, b)
    w1 = mlp_w1.astype(BF16)
    w2 = mlp_w2.astype(BF16)
    wo = na_wo.astype(BF16)
    h_lat = _post_block(a_lat, x, mod_lat, row(g_mlp[0]), wo, w1, w2, 0, tm=tm)
    h_ctx = _post_block(a_ctx, ctx_flat, mod_ctx, row(g_mlp[0]), wo, w1, w2, 0, tm=tmc)

    mod_lat, mod_ctx = mods_of(1)
    kvw = SWA_KV_HEADS * HEAD_DIM
    gq, gk = swa_q_gain[0], swa_k_gain[0]
    cos_t, sin_t = _rope_tables_t(l)
    rope_tabs = (_gain_rope_tables(gq * scale, cos_t, sin_t) + _gain_rope_tables(gk, cos_t, sin_t))
    qt, k, vt, wt = _qkv_project(h_lat, mod_lat, row(g_mix[1]), swa_wqkv[0], None, rope_tabs, kv_width=kvw,
                                 tm=TM_QKV)
    gains = (_gain_cols(gq, scale), _gain_cols(gk))
    _, kc, vct = _qkv_project(h_ctx, mod_ctx, row(g_mix[1]), wt, gains, None, kv_width=kvw, tm=tmc)
    sink_rows = jnp.repeat(swa_sink[0].astype(F32) * LOG2E, SWA_QB).reshape(SWA_KV_HEADS, 1, SWA_GROUP * SWA_QB)
    a_lat = _swa_attention(qt, k, vt, kc, vct, sink_rows)
    return _post_block(a_lat, h_lat, mod_lat, row(g_mlp[1]), swa_wo.astype(BF16), w1, w2, 1, tm=tm)
```
